```python
import math
import jax
import jax.numpy as jnp
from jax import lax
import numpy as np

D_MODEL = 1024
BATCH = 1
SEQ = 16384
DEPTH = 2
DEC_BATCH = 32
DEC_SEQ = 8
PAST_LEN = 16384
PAGE_SIZE = 128

S5_GROUP = 16
S5_GROUPS = D_MODEL // S5_GROUP
S5_STATE = 64
S5_CHUNK = 128
DT_MIN = 1e-3
DT_MAX = 1e-1
N_HEADS = 16
HEAD_DIM = D_MODEL // N_HEADS
N_KV = 4
HEADS_PER_KV = N_HEADS // N_KV
CMP_LEN = 32
CMP_STRIDE = 16
CMP_SPAN = CMP_LEN // CMP_STRIDE
CMP_HIDDEN = 2 * HEAD_DIM
SEL_BLOCK = 64
N_SEL = 16
WINDOW = 512
Q_BLOCK = 128
FORCE = 1e6
D_FF = ((8 * D_MODEL // 3 + 127) // 128) * 128
CONV_W = 3
ALPHA = (2.0 * DEPTH) ** 0.25
BETA = (8.0 * DEPTH) ** -0.25
LN_EPS = 1e-5
N_S5 = (DEPTH + 1) // 2
N_NSA = DEPTH // 2

kernel_name = 'hybrid_s5_nsa_convffn_step'


def layer_norm(x, g, b):
    xf = x.astype(jnp.float32)
    mu = jnp.mean(xf, -1, keepdims=True)
    var = jnp.mean(jnp.square(xf - mu), -1, keepdims=True)
    return ((xf - mu) * lax.rsqrt(var + LN_EPS) * g.astype(jnp.float32) + b.astype(jnp.float32)).astype(x.dtype)


def masked_softmax(s, mask):
    s = jnp.where(mask, s, -jnp.inf)
    mx = jnp.max(s, -1, keepdims=True)
    mx = jnp.where(jnp.isfinite(mx), mx, 0.0)
    e = jnp.where(mask, jnp.exp(s - mx), 0.0)
    den = jnp.sum(e, -1, keepdims=True)
    return e / jnp.where(den > 0, den, 1.0)


def _cmul(ar, ai, br, bi):
    return ar * br - ai * bi, ar * bi + ai * br


def _ssm_combine(e1, e2):
    a1r, a1i, b1r, b1i = e1
    a2r, a2i, b2r, b2i = e2
    ar, ai = _cmul(a2r, a2i, a1r, a1i)
    br, bim = _cmul(a2r, a2i, b1r, b1i)
    return ar, ai, br + b2r, bim + b2i


def s5_discretize(a_re, a_im, log_dt):
    f32 = jnp.float32
    dt = jnp.exp(log_dt.astype(f32))[:, None]
    a_re = a_re.astype(f32)
    a_im = a_im.astype(f32)
    mag = jnp.exp(a_re * dt)
    ab_re = mag * jnp.cos(a_im * dt)
    ab_im = mag * jnp.sin(a_im * dt)
    den = a_re * a_re + a_im * a_im
    nr = ab_re - 1.0
    f_re = (nr * a_re + ab_im * a_im) / den
    f_im = (ab_im * a_re - nr * a_im) / den
    return ab_re, ab_im, f_re, f_im


def s5_mixer(x, h0_re, h0_im, a_re, a_im, log_dt, b_re, b_im, c_re, c_im, d_skip, w_glu):
    f32 = jnp.float32
    bsz, L, _ = x.shape
    ab_re, ab_im, f_re, f_im = s5_discretize(a_re, a_im, log_dt)
    chunk = math.gcd(L, S5_CHUNK)
    u = x.astype(f32).reshape(bsz, L // chunk, chunk, S5_GROUPS, S5_GROUP).swapaxes(0, 1)
    a_r = jnp.broadcast_to(ab_re, (bsz, chunk, S5_GROUPS, S5_STATE))
    a_i = jnp.broadcast_to(ab_im, (bsz, chunk, S5_GROUPS, S5_STATE))
    b_re = b_re.astype(f32)
    b_im = b_im.astype(f32)
    c_re = c_re.astype(f32)
    c_im = c_im.astype(f32)

    def step(h, u_c):
        ur = jnp.einsum('bcgi,gpi->bcgp', u_c, b_re)
        ui = jnp.einsum('bcgi,gpi->bcgp', u_c, b_im)
        bu_r, bu_i = _cmul(f_re, f_im, ur, ui)
        hr, hi = _cmul(ab_re, ab_im, h[0], h[1])
        bu_r = bu_r.at[:, 0].add(hr)
        bu_i = bu_i.at[:, 0].add(hi)
        _, _, sr, si = lax.associative_scan(_ssm_combine, (a_r, a_i, bu_r, bu_i), axis=1)
        y = jnp.einsum('bcgp,gip->bcgi', sr, c_re) - jnp.einsum('bcgp,gip->bcgi', si, c_im)
        return (sr[:, -1], si[:, -1]), y

    (hr, hi), y = lax.scan(step, (h0_re.astype(f32), h0_im.astype(f32)), u)
    y = y.swapaxes(0, 1).reshape(bsz, L, D_MODEL) + x.astype(f32) * d_skip.astype(f32)
    z = jax.nn.gelu(y).astype(x.dtype)
    zz = z @ w_glu
    out = zz[..., :D_MODEL] * jax.nn.sigmoid(zz[..., D_MODEL:])
    return out, (hr, hi)


def nsa_project(x, w_in):
    bsz, L, _ = x.shape
    proj = x @ w_in
    hd = N_HEADS * HEAD_DIM
    kd = 6 * N_KV * HEAD_DIM
    q = proj[..., :hd].reshape(bsz, L, N_HEADS, HEAD_DIM)
    kv = proj[..., hd:hd + kd].reshape(bsz, L, 6, N_KV, HEAD_DIM)
    g = proj[..., hd + kd:].reshape(bsz, L, N_HEADS, 3)
    return q, kv, g


def nsa_compress(rows, pe, w1, b1, w2):
    bsz, L = rows.shape[:2]
    rows = jnp.pad(rows, ((0, 0), (0, (-L) % CMP_STRIDE), (0, 0), (0, 0)))
    n_chunk = rows.shape[1] // CMP_STRIDE
    ch = rows.reshape(bsz, n_chunk, CMP_STRIDE, N_KV, HEAD_DIM)
    n_blk = n_chunk - CMP_SPAN + 1
    h = b1
    for j in range(CMP_SPAN):
        sl = slice(j * CMP_STRIDE, (j + 1) * CMP_STRIDE)
        part = jnp.einsum('bnsgd,sdh->bngh', ch + pe[sl][:, None, :], w1[sl])
        h = h + part[:, j:j + n_blk]
    return jnp.einsum('bngh,hd->bngd', jax.nn.gelu(h), w2)


def cmp_to_sel(n_cmp, n_sel):
    i = jnp.arange(n_cmp)[:, None] * CMP_STRIDE
    j = jnp.arange(n_sel)[None, :] * SEL_BLOCK
    return ((i < j + SEL_BLOCK) & (i + CMP_LEN > j)).astype(jnp.float32)


def nsa_core(q, q_pos, kc, vc, sel_gather, n_sel_blocks, kw, vw, kw_pos, gates):
    f32 = jnp.float32
    bsz, nq = q.shape[:2]
    qg = q.reshape(bsz, nq, N_KV, HEADS_PER_KV, HEAD_DIM).astype(f32) * (HEAD_DIM ** -0.5)
    nc = kc.shape[1]
    c_end = jnp.arange(nc) * CMP_STRIDE + CMP_LEN - 1
    s_c = jnp.einsum('bqghd,bngd->bqghn', qg, kc.astype(f32))
    m_c = (c_end[None, :] <= q_pos[:, None])[None, :, None, None, :]
    p_c = masked_softmax(s_c, m_c)
    o_c = jnp.einsum('bqghn,bngd->bqghd', p_c, vc.astype(f32))
    imp = jnp.einsum('bqgn,nj->bqgj', jnp.sum(p_c, 3), cmp_to_sel(nc, n_sel_blocks))
    jj = jnp.arange(n_sel_blocks)[None, :]
    cur = (q_pos // SEL_BLOCK)[:, None]
    forced = ((jj == 0) | ((jj <= cur) & (jj >= cur - 1)))[None, :, None, :]
    future = (jj > cur)[None, :, None, :]
    imp = jnp.where(future, -FORCE, jnp.where(forced, FORCE, imp))
    _, blk = lax.top_k(imp, min(N_SEL, n_sel_blocks))
    ks, vs, kpos = sel_gather(blk)
    s_s = jnp.einsum('bqghd,bqgnrd->bqghnr', qg, ks.astype(f32))
    shp = s_s.shape
    m_s = (kpos <= q_pos[None, :, None, None, None]).reshape(bsz, nq, N_KV, 1, -1)
    p_s = masked_softmax(s_s.reshape(shp[:4] + (-1,)), m_s).reshape(shp)
    o_s = jnp.einsum('bqghnr,bqgnrd->bqghd', p_s, vs.astype(f32))
    s_w = jnp.einsum('bqghd,bwgd->bqghw', qg, kw.astype(f32))
    dlt = q_pos[:, None] - kw_pos[None, :]
    m_w = ((dlt >= 0) & (dlt < WINDOW) & (kw_pos >= 0)[None, :])[None, :, None, None, :]
    p_w = masked_softmax(s_w, m_w)
    o_w = jnp.einsum('bqghw,bwgd->bqghd', p_w, vw.astype(f32))
    g = jax.nn.sigmoid(gates.astype(f32)).reshape(bsz, nq, N_KV, HEADS_PER_KV, 3)
    o = g[..., 0:1] * o_c + g[..., 1:2] * o_s + g[..., 2:3] * o_w
    return o.reshape(bsz, nq, D_MODEL)


def nsa_prompt(x, w_in, pe, w1, b1, w2, w_o):
    bsz, seq, _ = x.shape
    q, kv, gates = nsa_project(x, w_in)
    k_c, v_c, k_s, v_s, k_w, v_w = [kv[:, :, i] for i in range(6)]
    kc = nsa_compress(k_c, pe[0], w1[0], b1[0], w2[0])
    vc = nsa_compress(v_c, pe[1], w1[1], b1[1], w2[1])
    n_sel_blocks = -(-seq // SEL_BLOCK)
    bi = jnp.arange(bsz)[:, None, None, None, None]
    gi = jnp.arange(N_KV)[None, None, :, None, None]
    r = jnp.arange(SEL_BLOCK)

    def sel_gather(blk):
        pos = blk[..., None] * SEL_BLOCK + r
        idx = jnp.minimum(pos, seq - 1)
        return k_s[bi, idx, gi], v_s[bi, idx, gi], pos

    pad = ((0, 0), (WINDOW, 0), (0, 0), (0, 0))
    kw_pad = jnp.pad(k_w, pad)
    vw_pad = jnp.pad(v_w, pad)
    qb = math.gcd(seq, Q_BLOCK)

    def block(i):
        q0 = i * qb
        q_pos = q0 + jnp.arange(qb)
        kw_pos = q0 - WINDOW + jnp.arange(WINDOW + qb)
        return nsa_core(lax.dynamic_slice_in_dim(q, q0, qb, 1), q_pos, kc, vc, sel_gather, n_sel_blocks,
                        lax.dynamic_slice_in_dim(kw_pad, q0, WINDOW + qb, 1),
                        lax.dynamic_slice_in_dim(vw_pad, q0, WINDOW + qb, 1), kw_pos,
                        lax.dynamic_slice_in_dim(gates, q0, qb, 1))

    o = lax.map(block, jnp.arange(seq // qb))
    o = jnp.moveaxis(o, 0, 1).reshape(bsz, seq, D_MODEL)
    y = o.astype(x.dtype) @ w_o
    win = min(WINDOW, seq)
    return y, (k_c, v_c, k_s, v_s, k_w[:, seq - win:], v_w[:, seq - win:])


def nsa_sample(x, li, page_table, pool_kc, pool_vc, pool_ks, pool_vs, buf_kw, buf_vw,
               w_in, pe, w1, b1, w2, w_o):
    bsz, t, _ = x.shape
    past = page_table.shape[1] * PAGE_SIZE
    total = past + t
    q, kv, gates = nsa_project(x, w_in)
    k_c, v_c, k_s, v_s, k_w, v_w = [kv[:, :, i] for i in range(6)]

    def all_rows(pool, new):
        old = pool[li, page_table].reshape(bsz, past, N_KV, HEAD_DIM)
        return jnp.concatenate([old.astype(new.dtype), new], axis=1)

    kc = nsa_compress(all_rows(pool_kc, k_c), pe[0], w1[0], b1[0], w2[0])
    vc = nsa_compress(all_rows(pool_vc, v_c), pe[1], w1[1], b1[1], w2[1])
    bi = jnp.arange(bsz)[:, None, None, None, None]
    gi = jnp.arange(N_KV)[None, None, :, None, None]
    r = jnp.arange(SEL_BLOCK)

    def sel_gather(blk):
        pos = blk[..., None] * SEL_BLOCK + r
        in_past = (pos < past)[..., None]
        pp = jnp.minimum(pos, past - 1)
        phys = page_table[bi, pp // PAGE_SIZE]
        off = pp % PAGE_SIZE
        pn = jnp.clip(pos - past, 0, t - 1)

        def take(pool, new):
            return jnp.where(in_past, pool[li, phys, off, gi].astype(new.dtype), new[bi, pn, gi])

        return take(pool_ks, k_s), take(pool_vs, v_s), pos

    wb = buf_kw.shape[2]
    kw_all = jnp.concatenate([buf_kw[li].astype(k_w.dtype), k_w], axis=1)
    vw_all = jnp.concatenate([buf_vw[li].astype(v_w.dtype), v_w], axis=1)
    kw_pos = past - wb + jnp.arange(wb + t)
    q_pos = past + jnp.arange(t)
    o = nsa_core(q, q_pos, kc, vc, sel_gather, -(-total // SEL_BLOCK), kw_all, vw_all, kw_pos, gates)
    y = o.astype(x.dtype) @ w_o
    keep = min(WINDOW, wb + t)
    return y, (k_c, v_c, k_s, v_s, kw_all[:, wb + t - keep:], vw_all[:, wb + t - keep:])


def conv_ffn(x, buf, w_up, conv_w, conv_b, w_down):
    L = x.shape[1]
    h = x @ w_up
    hp = jnp.concatenate([buf.astype(h.dtype), h], axis=1)
    hc = conv_b
    for k in range(CONV_W):
        hc = hc + conv_w[k] * hp[:, k:k + L]
    a, v = jnp.split(hc, 2, axis=-1)
    return (jax.nn.silu(a) * v) @ w_down, hp[:, L:]


def setup_inputs(seed: int = 0) -> dict:
    key = jax.random.key(seed)
    keys = iter(jax.random.split(key, 48))
    f32 = jnp.float32

    def nrm(shape, scale=1.0):
        return jax.random.normal(next(keys), shape, f32) * scale

    n_pages = PAST_LEN // PAGE_SIZE
    used = DEC_BATCH * n_pages
    n_pool = used + max(1, used // 4)
    wb = min(WINDOW, PAST_LEN)
    kvd = N_KV * HEAD_DIM
    inp = {}
    inp['x_prompt'] = nrm((BATCH, SEQ, D_MODEL))
    inp['x_sample'] = nrm((DEC_BATCH, DEC_SEQ, D_MODEL))
    inp['state_s5_re'] = nrm((N_S5, DEC_BATCH, S5_GROUPS, S5_STATE), 0.3)
    inp['state_s5_im'] = nrm((N_S5, DEC_BATCH, S5_GROUPS, S5_STATE), 0.3)
    pool_shape = (N_NSA, n_pool, PAGE_SIZE, N_KV, HEAD_DIM)
    inp['cache_k_cmp'] = nrm(pool_shape)
    inp['cache_v_cmp'] = nrm(pool_shape, BETA)
    inp['cache_k_sel'] = nrm(pool_shape)
    inp['cache_v_sel'] = nrm(pool_shape, BETA)
    inp['cache_k_win'] = nrm((N_NSA, DEC_BATCH, wb, N_KV, HEAD_DIM))
    inp['cache_v_win'] = nrm((N_NSA, DEC_BATCH, wb, N_KV, HEAD_DIM), BETA)
    inp['state_ffn_conv'] = nrm((DEPTH, DEC_BATCH, CONV_W - 1, 2 * D_FF))
    perm = jax.random.permutation(next(keys), n_pool)[:used]
    inp['page_table'] = perm.reshape(DEC_BATCH, n_pages).astype(jnp.int32)
    n_idx = jnp.arange(S5_STATE, dtype=f32)
    inp['s5_a_re'] = -0.5 + nrm((N_S5, S5_GROUPS, S5_STATE), 0.01)
    inp['s5_a_im'] = math.pi * n_idx + nrm((N_S5, S5_GROUPS, S5_STATE), 0.01)
    inp['s5_log_dt'] = jax.random.uniform(next(keys), (N_S5, S5_GROUPS), f32, math.log(DT_MIN), math.log(DT_MAX))
    inp['s5_b_re'] = nrm((N_S5, S5_GROUPS, S5_STATE, S5_GROUP), (2.0 * S5_GROUP) ** -0.5)
    inp['s5_b_im'] = nrm((N_S5, S5_GROUPS, S5_STATE, S5_GROUP), (2.0 * S5_GROUP) ** -0.5)
    inp['s5_c_re'] = nrm((N_S5, S5_GROUPS, S5_GROUP, S5_STATE), (2.0 * S5_STATE) ** -0.5)
    inp['s5_c_im'] = nrm((N_S5, S5_GROUPS, S5_GROUP, S5_STATE), (2.0 * S5_STATE) ** -0.5)
    inp['s5_d'] = nrm((N_S5, D_MODEL))
    glu_scale = jnp.concatenate([jnp.full((D_MODEL,), BETA, f32), jnp.ones((D_MODEL,), f32)])
    inp['s5_w_glu'] = nrm((N_S5, D_MODEL, 2 * D_MODEL), D_MODEL ** -0.5) * glu_scale
    one, bet = jnp.ones((kvd,), f32), jnp.full((kvd,), BETA, f32)
    in_scale = jnp.concatenate([jnp.ones((N_HEADS * HEAD_DIM,), f32), one, bet, one, bet, one, bet,
                                jnp.ones((3 * N_HEADS,), f32)])
    n_in = N_HEADS * HEAD_DIM + 6 * kvd + 3 * N_HEADS
    inp['nsa_w_in'] = nrm((N_NSA, D_MODEL, n_in), D_MODEL ** -0.5) * in_scale
    inp['nsa_cmp_pe'] = nrm((N_NSA, 2, CMP_LEN, HEAD_DIM), 0.1)
    inp['nsa_cmp_w1'] = nrm((N_NSA, 2, CMP_LEN, HEAD_DIM, CMP_HIDDEN), (CMP_LEN * HEAD_DIM) ** -0.5)
    inp['nsa_cmp_b1'] = nrm((N_NSA, 2, CMP_HIDDEN), 0.01)
    inp['nsa_cmp_w2'] = nrm((N_NSA, 2, CMP_HIDDEN, HEAD_DIM), CMP_HIDDEN ** -0.5)
    inp['nsa_w_o'] = nrm((N_NSA, D_MODEL, D_MODEL), BETA * D_MODEL ** -0.5)
    inp['ffn_w_up'] = nrm((DEPTH, D_MODEL, 2 * D_FF), D_MODEL ** -0.5)
    inp['ffn_conv_w'] = nrm((DEPTH, CONV_W, 2 * D_FF), CONV_W ** -0.5)
    inp['ffn_conv_b'] = nrm((DEPTH, 2 * D_FF), 0.01)
    inp['ffn_w_down'] = nrm((DEPTH, D_FF, D_MODEL), BETA * D_FF ** -0.5)
    inp['ln_mix_g'] = 1.0 + nrm((DEPTH, D_MODEL), 0.01)
    inp['ln_mix_b'] = nrm((DEPTH, D_MODEL), 0.01)
    inp['ln_ffn_g'] = 1.0 + nrm((DEPTH, D_MODEL), 0.01)
    inp['ln_ffn_b'] = nrm((DEPTH, D_MODEL), 0.01)
    return inp


def reference(x_prompt, x_sample, state_s5_re, state_s5_im, cache_k_cmp, cache_v_cmp, cache_k_sel, cache_v_sel,
              cache_k_win, cache_v_win, state_ffn_conv, page_table,
              s5_a_re, s5_a_im, s5_log_dt, s5_b_re, s5_b_im, s5_c_re, s5_c_im, s5_d, s5_w_glu,
              nsa_w_in, nsa_cmp_pe, nsa_cmp_w1, nsa_cmp_b1, nsa_cmp_w2, nsa_w_o,
              ffn_w_up, ffn_conv_w, ffn_conv_b, ffn_w_down,
              ln_mix_g, ln_mix_b, ln_ffn_g, ln_ffn_b):
    xp, xs = x_prompt, x_sample
    s5_p, s5_s, nsa_p, nsa_s, conv_p, conv_s = [], [], [], [], [], []
    for layer in range(DEPTH):
        li = layer // 2
        if layer % 2 == 0:
            s5_w = (s5_a_re[li], s5_a_im[li], s5_log_dt[li], s5_b_re[li], s5_b_im[li],
                    s5_c_re[li], s5_c_im[li], s5_d[li], s5_w_glu[li])
            h0 = jnp.zeros((xp.shape[0], S5_GROUPS, S5_STATE), jnp.float32)
            mp, st_p = s5_mixer(xp, h0, h0, *s5_w)
            ms, st_s = s5_mixer(xs, state_s5_re[li], state_s5_im[li], *s5_w)
            s5_p.append(st_p)
            s5_s.append(st_s)
        else:
            nsa_w = (nsa_w_in[li], nsa_cmp_pe[li], nsa_cmp_w1[li], nsa_cmp_b1[li], nsa_cmp_w2[li], nsa_w_o[li])
            mp, rows_p = nsa_prompt(xp, *nsa_w)
            ms, rows_s = nsa_sample(xs, li, page_table, cache_k_cmp, cache_v_cmp, cache_k_sel, cache_v_sel,
                                    cache_k_win, cache_v_win, *nsa_w)
            nsa_p.append(rows_p)
            nsa_s.append(rows_s)
        xp = layer_norm(ALPHA * xp + mp, ln_mix_g[layer], ln_mix_b[layer])
        xs = layer_norm(ALPHA * xs + ms, ln_mix_g[layer], ln_mix_b[layer])
        ffn_w = (ffn_w_up[layer], ffn_conv_w[layer], ffn_conv_b[layer], ffn_w_down[layer])
        buf0 = jnp.zeros((xp.shape[0], CONV_W - 1, 2 * D_FF), xp.dtype)
        fp, cp = conv_ffn(xp, buf0, *ffn_w)
        fs, cs = conv_ffn(xs, state_ffn_conv[layer], *ffn_w)
        conv_p.append(cp)
        conv_s.append(cs)
        xp = layer_norm(ALPHA * xp + fp, ln_ffn_g[layer], ln_ffn_b[layer])
        xs = layer_norm(ALPHA * xs + fs, ln_ffn_g[layer], ln_ffn_b[layer])

    def stk(lst, i):
        return jnp.stack([e[i] for e in lst])

    return (xp, xs,
            stk(s5_p, 0), stk(s5_p, 1), stk(s5_s, 0), stk(s5_s, 1),
            stk(nsa_p, 0), stk(nsa_p, 1), stk(nsa_p, 2), stk(nsa_p, 3),
            stk(nsa_s, 0), stk(nsa_s, 1), stk(nsa_s, 2), stk(nsa_s, 3),
            stk(nsa_p, 4), stk(nsa_p, 5), stk(nsa_s, 4), stk(nsa_s, 5),
            jnp.stack(conv_p), jnp.stack(conv_s))
```

```python
import functools
import math

import jax
import jax.numpy as jnp
from jax import lax
from jax.experimental import pallas as pl
from jax.experimental.pallas import tpu as pltpu

F32 = jnp.float32
BF16 = jnp.bfloat16

D_MODEL = 1024
S5_GROUP = 16
S5_GROUPS = D_MODEL // S5_GROUP
S5_STATE = 64
S5_T = 16
N_HEADS = 16
HEAD_DIM = 64
N_KV = 4
HEADS_PER_KV = N_HEADS // N_KV
KV_DIM = N_KV * HEAD_DIM
CMP_LEN = 32
CMP_STRIDE = 16
CMP_HIDDEN = 2 * HEAD_DIM
SEL_BLOCK = 64
N_SEL = 16
WINDOW = 512
PAGE_SIZE = 128
FORCE = 1e6
D_FF = 2816
CONV_W = 3
FFN_CHUNK = 256
DEPTH = 2
ALPHA = (2.0 * DEPTH) ** 0.25
LN_EPS = 1e-5
NEG_BIG = -1e30

VMEM_LIMIT = 56 * 1024 * 1024


def _cparams(*sem):
    return pltpu.CompilerParams(dimension_semantics=sem, vmem_limit_bytes=VMEM_LIMIT)


def _layer_norm(r, g, b):
    mu = jnp.mean(r, -1, keepdims=True)
    c = r - mu
    var = jnp.mean(c * c, -1, keepdims=True)
    return c * lax.rsqrt(var + LN_EPS) * g + b


def _swap(x):
    return pltpu.roll(x, S5_STATE, axis=x.ndim - 1)


def _lo_mask(shape):
    return lax.broadcasted_iota(jnp.int32, shape, len(shape) - 1) < S5_STATE


def _cmul(x, w):
    ws = _swap(w)
    lo = _lo_mask(w.shape)
    wr = jnp.where(lo, w, ws)
    wi = jnp.where(lo, -ws, w)
    return x * wr + _swap(x) * wi


def _s5_prep_kernel(a_ref, dt_ref, bt_ref, btt_ref, ct_ref, m_ref, w_ref, v_ref, z_ref):
    P = S5_STATE
    T = S5_T
    a = a_ref[0]
    dt = jnp.exp(dt_ref[0])
    lo = _lo_mask(a.shape)
    a_sw = _swap(a)
    are = jnp.where(lo, a, a_sw)
    aim = jnp.where(lo, a_sw, a)
    mag = jnp.exp(are * dt)
    ang = aim * dt
    ab = mag * jnp.where(lo, jnp.cos(ang), jnp.sin(ang))
    ab_sw = _swap(ab)
    abr = jnp.where(lo, ab, ab_sw)
    abi = jnp.where(lo, ab_sw, ab)
    den = are * are + aim * aim
    nr = abr - 1.0
    f = jnp.where(lo, nr * are + abi * aim, abi * are - nr * aim) / den

    pw = [ab]
    for _ in range(4):
        pw.append(_cmul(pw[-1], pw[-1]))
    z_ref[0] = jnp.concatenate(
        [jnp.broadcast_to(pw[3], (4, 2 * P)), jnp.broadcast_to(pw[4], (4, 2 * P))], axis=0)

    rows = lax.broadcasted_iota(jnp.int32, (T * S5_GROUP, 2 * P), 0) // S5_GROUP
    e = (T - 1) - rows
    one = jnp.where(_lo_mask((T * S5_GROUP, 2 * P)), 1.0, 0.0).astype(F32)
    apow = one
    for b in range(4):
        apow = jnp.where(((e >> b) & 1) == 1, _cmul(apow, pw[b]), apow)
    af = _cmul(apow, f)
    w_ref[0] = _cmul(af, btt_ref[0])

    col = [jnp.transpose(jnp.broadcast_to(p, (2 * P, 2 * P))) for p in pw[:4]]
    lane_e = lax.broadcasted_iota(jnp.int32, (P, T * S5_GROUP), 1) // S5_GROUP

    def col_cmul(xr, xi, c):
        cr = jnp.concatenate([c[:P], c[:P]], axis=1)
        ci = jnp.concatenate([c[P:], c[P:]], axis=1)
        return xr * cr - xi * ci, xr * ci + xi * cr

    pr = jnp.ones((P, T * S5_GROUP), F32)
    pi = jnp.zeros((P, T * S5_GROUP), F32)
    for b in range(4):
        nr_, ni_ = col_cmul(pr, pi, col[b])
        sel = ((lane_e >> b) & 1) == 1
        pr = jnp.where(sel, nr_, pr)
        pi = jnp.where(sel, ni_, pi)
    ctr = ct_ref[0, 0]
    cti = ct_ref[0, 1]
    qr = pr * ctr - pi * cti
    qi = pr * cti + pi * ctr
    q1r, q1i = col_cmul(qr, qi, col[0])
    v_ref[0] = jnp.concatenate([q1r, -q1i], axis=0).astype(v_ref.dtype)

    fr = jnp.where(lo, f, _swap(f))[:, :P]
    fi = jnp.where(lo, _swap(f), f)[:, :P]
    btr = bt_ref[0, 0]
    bti = bt_ref[0, 1]
    bfr = btr * fr - bti * fi
    bfi = btr * fi + bti * fr
    hp = lax.Precision.HIGHEST
    kk = (jnp.dot(bfr, qr, precision=hp, preferred_element_type=F32)
          - jnp.dot(bfi, qi, precision=hp, preferred_element_type=F32))
    lane = lax.broadcasted_iota(jnp.int32, kk.shape, 1)
    for s in range(T):
        blk = kk if s == 0 else jnp.where(lane >= S5_GROUP * s, pltpu.roll(kk, S5_GROUP * s, axis=1), 0.0)
        m_ref[0, S5_GROUP * s:S5_GROUP * (s + 1), :] = blk.astype(m_ref.dtype)


def _s5_prep(a_re, a_im, log_dt, b_re, b_im, c_re, c_im):
    G, P, T = S5_GROUPS, S5_STATE, S5_T
    a_pk = jnp.concatenate([a_re, a_im], axis=-1).reshape(G, 1, 2 * P)
    dt_pk = jnp.broadcast_to(log_dt.reshape(G, 1, 1), (G, 1, 2 * P))
    bt = jnp.stack([b_re, b_im], axis=1).transpose(0, 1, 3, 2)
    btt = jnp.tile(jnp.concatenate([b_re, b_im], axis=1).transpose(0, 2, 1), (1, T, 1))
    ct = jnp.tile(jnp.stack([c_re, c_im], axis=1).transpose(0, 1, 3, 2), (1, 1, 1, T))
    n = T * S5_GROUP
    return pl.pallas_call(
        _s5_prep_kernel,
        grid=(G,),
        in_specs=[
            pl.BlockSpec((1, 1, 2 * P), lambda g: (g, 0, 0)),
            pl.BlockSpec((1, 1, 2 * P), lambda g: (g, 0, 0)),
            pl.BlockSpec((1, 2, S5_GROUP, P), lambda g: (g, 0, 0, 0)),
            pl.BlockSpec((1, n, 2 * P), lambda g: (g, 0, 0)),
            pl.BlockSpec((1, 2, P, n), lambda g: (g, 0, 0, 0)),
        ],
        out_specs=[
            pl.BlockSpec((1, n, n), lambda g: (g, 0, 0)),
            pl.BlockSpec((1, n, 2 * P), lambda g: (g, 0, 0)),
            pl.BlockSpec((1, 2 * P, n), lambda g: (g, 0, 0)),
            pl.BlockSpec((1, 8, 2 * P), lambda g: (g, 0, 0)),
        ],
        out_shape=[
            jax.ShapeDtypeStruct((G, n, n), BF16),
            jax.ShapeDtypeStruct((G, n, 2 * P), F32),
            jax.ShapeDtypeStruct((G, 2 * P, n), BF16),
            jax.ShapeDtypeStruct((G, 8, 2 * P), F32),
        ],
        compiler_params=_cparams("parallel"),
        name="s5_prep",
    )(a_pk, dt_pk, bt, btt, ct)


def _s5_scan_kernel(u_ref, m_ref, w_ref, v_ref, z_ref, y_ref, hf_ref):
    u = u_ref[0]
    C = u.shape[0]
    s = jnp.dot(u, w_ref[0].astype(BF16), preferred_element_type=F32)
    z = z_ref[0, 4:5, :]
    row = lax.broadcasted_iota(jnp.int32, s.shape, 0)
    h = s
    d = 1
    while d < C:
        hs = jnp.where(row >= d, pltpu.roll(h, d, axis=0), 0.0)
        h = h + _cmul(hs, z)
        z = _cmul(z, z)
        d *= 2
    hprev = jnp.where(row >= 1, pltpu.roll(h, 1, axis=0), 0.0)
    y = (jnp.dot(u, m_ref[0], preferred_element_type=F32)
         + jnp.dot(hprev.astype(BF16), v_ref[0], preferred_element_type=F32))
    y_ref[0] = y.astype(y_ref.dtype)
    hf_ref[0] = h[C - 8:, :]


def _s5_step_kernel(u_ref, h0_ref, m_ref, w_ref, v_ref, z_ref, y_ref, hf_ref, *, t_len):
    n = t_len * S5_GROUP
    off = (S5_T - t_len) * S5_GROUP
    u = u_ref[0]
    h0 = h0_ref[0]
    hp = lax.Precision.HIGHEST
    s = jnp.dot(u, w_ref[0, off:, :], precision=hp, preferred_element_type=F32)
    z = z_ref[0, 0:1, :] if t_len * 2 == S5_T else z_ref[0, 4:5, :]
    hf_ref[0] = _cmul(h0, z) + s
    y = (jnp.dot(u.astype(BF16), m_ref[0, :n, :n], preferred_element_type=F32)
         + jnp.dot(h0.astype(BF16), v_ref[0, :, :n], preferred_element_type=F32))
    y_ref[0] = y.astype(y_ref.dtype)


def _s5_core_prompt(x, mats):
    m, w, v, z = mats
    L = x.shape[0]
    G, T = S5_GROUPS, S5_T
    C = L // T
    n = T * S5_GROUP
    u = x.reshape(C, T, G, S5_GROUP).transpose(2, 0, 1, 3).reshape(G, C, n).astype(BF16)
    y, hf = pl.pallas_call(
        _s5_scan_kernel,
        grid=(G,),
        in_specs=[
            pl.BlockSpec((1, C, n), lambda g: (g, 0, 0)),
            pl.BlockSpec((1, n, n), lambda g: (g, 0, 0)),
            pl.BlockSpec((1, n, 2 * S5_STATE), lambda g: (g, 0, 0)),
            pl.BlockSpec((1, 2 * S5_STATE, n), lambda g: (g, 0, 0)),
            pl.BlockSpec((1, 8, 2 * S5_STATE), lambda g: (g, 0, 0)),
        ],
        out_specs=[
            pl.BlockSpec((1, C, n), lambda g: (g, 0, 0)),
            pl.BlockSpec((1, 8, 2 * S5_STATE), lambda g: (g, 0, 0)),
        ],
        out_shape=[
            jax.ShapeDtypeStruct((G, C, n), BF16),
            jax.ShapeDtypeStruct((G, 8, 2 * S5_STATE), F32),
        ],
        compiler_params=_cparams("parallel"),
        name="s5_scan",
    )(u, m, w, v, z)
    y = y.reshape(G, C, T, S5_GROUP).transpose(1, 2, 0, 3).reshape(L, D_MODEL)
    return y, hf[:, 7, :]


def _s5_core_sample(x, h_re, h_im, mats):
    m, w, v, z = mats
    B, t_len, _ = x.shape
    assert t_len in (S5_T // 2, S5_T)
    G = S5_GROUPS
    n = t_len * S5_GROUP
    u = x.reshape(B, t_len, G, S5_GROUP).transpose(2, 0, 1, 3).reshape(G, B, n)
    h0 = jnp.concatenate([h_re, h_im], axis=-1).transpose(1, 0, 2)
    nf = S5_T * S5_GROUP
    y, hf = pl.pallas_call(
        functools.partial(_s5_step_kernel, t_len=t_len),
        grid=(G,),
        in_specs=[
            pl.BlockSpec((1, B, n), lambda g: (g, 0, 0)),
            pl.BlockSpec((1, B, 2 * S5_STATE), lambda g: (g, 0, 0)),
            pl.BlockSpec((1, nf, nf), lambda g: (g, 0, 0)),
            pl.BlockSpec((1, nf, 2 * S5_STATE), lambda g: (g, 0, 0)),
            pl.BlockSpec((1, 2 * S5_STATE, nf), lambda g: (g, 0, 0)),
            pl.BlockSpec((1, 8, 2 * S5_STATE), lambda g: (g, 0, 0)),
        ],
        out_specs=[
            pl.BlockSpec((1, B, n), lambda g: (g, 0, 0)),
            pl.BlockSpec((1, B, 2 * S5_STATE), lambda g: (g, 0, 0)),
        ],
        out_shape=[
            jax.ShapeDtypeStruct((G, B, n), BF16),
            jax.ShapeDtypeStruct((G, B, 2 * S5_STATE), F32),
        ],
        compiler_params=_cparams("parallel"),
        name="s5_step",
    )(u, h0, m, w, v, z)
    y = y.reshape(G, B, t_len, S5_GROUP).transpose(1, 2, 0, 3).reshape(B * t_len, D_MODEL)
    return y, hf


def _s5_glu_ln_kernel(x_ref, y_ref, d_ref, w_ref, g_ref, b_ref, o_ref):
    x = x_ref[...]
    yy = y_ref[...].astype(F32) + x * d_ref[...]
    zg = jax.nn.gelu(yy).astype(BF16)
    zz = jnp.dot(zg, w_ref[...], preferred_element_type=F32)
    out = zz[:, :D_MODEL] * jax.nn.sigmoid(zz[:, D_MODEL:])
    o_ref[...] = _layer_norm(ALPHA * x + out, g_ref[...], b_ref[...])


def _row_tile(n, cap):
    t = min(n, cap)
    while n % t:
        t //= 2
    return t


def _s5_glu_ln(x, y, d_skip, w_glu, g, b):
    n = x.shape[0]
    tm = _row_tile(n, 512)
    const = lambda i: (0, 0)
    return pl.pallas_call(
        _s5_glu_ln_kernel,
        grid=(n // tm,),
        in_specs=[
            pl.BlockSpec((tm, D_MODEL), lambda i: (i, 0)),
            pl.BlockSpec((tm, D_MODEL), lambda i: (i, 0)),
            pl.BlockSpec((1, D_MODEL), const),
            pl.BlockSpec((D_MODEL, 2 * D_MODEL), const),
            pl.BlockSpec((1, D_MODEL), const),
            pl.BlockSpec((1, D_MODEL), const),
        ],
        out_specs=pl.BlockSpec((tm, D_MODEL), lambda i: (i, 0)),
        out_shape=jax.ShapeDtypeStruct((n, D_MODEL), F32),
        compiler_params=_cparams("parallel"),
        name="s5_glu_ln",
    )(x, y, d_skip.reshape(1, -1), w_glu.astype(BF16), g.reshape(1, -1), b.reshape(1, -1))


def _ffn_kernel(x_ref, p1_ref, p2_ref, wu_ref, cw_ref, cb_ref, wd_ref, g_ref, b_ref,
                o_ref, st_ref, carry_ref, *, period, chained):
    i = pl.program_id(0)
    x = x_ref[...]
    xb = x.astype(BF16)
    tm = x.shape[0]
    row = lax.broadcasted_iota(jnp.int32, (tm, 1), 0)
    t = row % period
    if chained:
        @pl.when(i == 0)
        def _():
            carry_ref[...] = jnp.zeros_like(carry_ref)

    def conv(cols):
        h = jnp.dot(xb, wu_ref[:, cols], preferred_element_type=F32)
        h1 = jnp.where(t >= 1, pltpu.roll(h, 1, axis=0), 0.0)
        h2 = jnp.where(t >= 2, pltpu.roll(h, 2, axis=0), 0.0)
        if chained:
            c0 = carry_ref[6:7, cols]
            c1 = carry_ref[7:8, cols]
            h1 = h1 + jnp.where(row == 0, c1, 0.0)
            h2 = h2 + jnp.where(row == 0, c0, 0.0) + jnp.where(row == 1, c1, 0.0)
            carry_ref[:, cols] = h[tm - 8:, :]
            st_ref[:, cols] = h[tm - 8:, :]
        else:
            h1 = h1 + p1_ref[:, cols]
            h2 = h2 + p2_ref[:, cols]
            st_ref[:, cols] = h
        return cb_ref[:, cols] + cw_ref[0:1, cols] * h2 + cw_ref[1:2, cols] * h1 + cw_ref[2:3, cols] * h

    f = jnp.zeros((tm, D_MODEL), F32)
    for c in range(D_FF // FFN_CHUNK):
        ca = slice(c * FFN_CHUNK, (c + 1) * FFN_CHUNK)
        cv = slice(D_FF + c * FFN_CHUNK, D_FF + (c + 1) * FFN_CHUNK)
        act = jax.nn.silu(conv(ca)) * conv(cv)
        f = f + jnp.dot(act.astype(BF16), wd_ref[ca, :], preferred_element_type=F32)
    o_ref[...] = _layer_norm(ALPHA * x + f, g_ref[...], b_ref[...])


def _ffn(x, buf, w_up, conv_w, conv_b, w_down, g, b, *, seq_len):
    n = x.shape[0]
    n_seq = n // seq_len
    f2 = 2 * D_FF
    chained = buf is None
    if chained:
        assert n_seq == 1
        tm = _row_tile(n, 512)
        p1 = p2 = jnp.zeros((8, f2), F32)
        pspec = pl.BlockSpec((8, f2), lambda i: (0, 0))
        st_shape, st_spec = (8, f2), pl.BlockSpec((8, f2), lambda i: (0, 0))
    else:
        tm = n
        z = jnp.zeros((n_seq, seq_len, f2), F32)
        p1 = z.at[:, 0].set(buf[:, 1]).reshape(n, f2)
        p2 = z.at[:, 0].set(buf[:, 0]).at[:, 1].set(buf[:, 1]).reshape(n, f2)
        pspec = pl.BlockSpec((tm, f2), lambda i: (i, 0))
        st_shape, st_spec = (n, f2), pl.BlockSpec((tm, f2), lambda i: (i, 0))
    const = lambda i: (0, 0)
    out, st = pl.pallas_call(
        functools.partial(_ffn_kernel, period=seq_len, chained=chained),
        grid=(n // tm,),
        in_specs=[
            pl.BlockSpec((tm, D_MODEL), lambda i: (i, 0)),
            pspec, pspec,
            pl.BlockSpec((D_MODEL, f2), const, pipeline_mode=pl.Buffered(1)),
            pl.BlockSpec((CONV_W, f2), const),
            pl.BlockSpec((1, f2), const),
            pl.BlockSpec((D_FF, D_MODEL), const, pipeline_mode=pl.Buffered(1)),
            pl.BlockSpec((1, D_MODEL), const),
            pl.BlockSpec((1, D_MODEL), const),
        ],
        out_specs=[pl.BlockSpec((tm, D_MODEL), lambda i: (i, 0)), st_spec],
        out_shape=[jax.ShapeDtypeStruct((n, D_MODEL), F32), jax.ShapeDtypeStruct(st_shape, F32)],
        scratch_shapes=[pltpu.VMEM((8, f2), F32)],
        compiler_params=_cparams("arbitrary"),
        name="conv_ffn",
    )(x, p1, p2, w_up.astype(BF16), conv_w, conv_b.reshape(1, -1), w_down.astype(BF16),
      g.reshape(1, -1), b.reshape(1, -1))
    if chained:
        state = st[6:8][None]
    else:
        state = st.reshape(n_seq, seq_len, f2)[:, seq_len - 2:]
    return out, state


TQ = 128
TK = 512


def _perm_heads():
    idx = jnp.arange(D_MODEL).reshape(N_KV, HEADS_PER_KV, HEAD_DIM)
    return idx.transpose(1, 0, 2).reshape(-1)


def _nsa_proj_kernel(x_ref, wq_ref, wkv_ref, wg_ref, q_ref, kc_ref, vc_ref, ks_ref, vs_ref, kw_ref, vw_ref,
                     kvb_ref, g_ref):
    xb = x_ref[...].astype(BF16)
    q = jnp.dot(xb, wq_ref[...], preferred_element_type=F32) * (HEAD_DIM ** -0.5)
    q_ref[...] = q.astype(BF16)
    kv = jnp.dot(xb, wkv_ref[...], preferred_element_type=F32)
    for j, r in enumerate((kc_ref, vc_ref, ks_ref, vs_ref, kw_ref, vw_ref)):
        r[...] = kv[:, j * KV_DIM:(j + 1) * KV_DIM]
    kvb_ref[...] = kv[:, 2 * KV_DIM:].astype(BF16)
    g_ref[...] = jax.nn.sigmoid(jnp.dot(xb, wg_ref[...], preferred_element_type=F32))


def _nsa_proj(x, w_in):
    n = x.shape[0]
    tm = _row_tile(n, 512)
    hd = N_HEADS * HEAD_DIM
    wq = w_in[:, :hd][:, _perm_heads()].astype(BF16)
    wkv = w_in[:, hd:hd + 6 * KV_DIM].astype(BF16)
    wg = jnp.pad(w_in[:, hd + 6 * KV_DIM:], ((0, 0), (0, 128 - 3 * N_HEADS))).astype(BF16)
    const = lambda i: (0, 0)
    rows = lambda w: pl.BlockSpec((tm, w), lambda i: (i, 0))
    return pl.pallas_call(
        _nsa_proj_kernel,
        grid=(n // tm,),
        in_specs=[rows(D_MODEL), pl.BlockSpec((D_MODEL, hd), const), pl.BlockSpec((D_MODEL, 6 * KV_DIM), const),
                  pl.BlockSpec((D_MODEL, 128), const)],
        out_specs=[rows(hd)] + [rows(KV_DIM)] * 6 + [rows(4 * KV_DIM), rows(128)],
        out_shape=[jax.ShapeDtypeStruct((n, hd), BF16)] + [jax.ShapeDtypeStruct((n, KV_DIM), F32)] * 6
        + [jax.ShapeDtypeStruct((n, 4 * KV_DIM), BF16), jax.ShapeDtypeStruct((n, 128), F32)],
        compiler_params=_cparams("parallel"),
        name="nsa_proj",
    )(x, wq, wkv, wg)


def _cmp_weights(pe, w1, b1, w2):
    eye = jnp.eye(N_KV, dtype=F32)
    w1b = jnp.einsum('jsdh,ge->jsgdeh', w1.reshape(2, CMP_STRIDE, HEAD_DIM, CMP_HIDDEN), eye)
    w1b = w1b.reshape(2, CMP_STRIDE * KV_DIM, N_KV * CMP_HIDDEN).astype(BF16)
    w2b = jnp.einsum('hd,ge->ghed', w2, eye).reshape(N_KV * CMP_HIDDEN, KV_DIM).astype(BF16)
    peb = jnp.broadcast_to(pe.reshape(2, CMP_STRIDE, 1, HEAD_DIM), (2, CMP_STRIDE, N_KV, HEAD_DIM))
    peb = peb.reshape(2, 1, CMP_STRIDE * KV_DIM)
    b1b = jnp.tile(b1, N_KV).reshape(1, N_KV * CMP_HIDDEN)
    return peb, w1b, b1b, w2b


def _compress_rows(x, carry_ref, pe_ref, w1_ref, b1_ref, w2_ref):
    p0 = jnp.dot((x + pe_ref[0]).astype(BF16), w1_ref[0], preferred_element_type=F32)
    p1 = jnp.dot((x + pe_ref[1]).astype(BF16), w1_ref[1], preferred_element_type=F32)
    row = lax.broadcasted_iota(jnp.int32, (x.shape[0], 1), 0)
    p0s = jnp.where(row == 0, carry_ref[7:8, :], pltpu.roll(p0, 1, axis=0))
    carry_ref[...] = p0[x.shape[0] - 8:, :]
    h = b1_ref[...] + p0s + p1
    return jnp.dot(jax.nn.gelu(h).astype(BF16), w2_ref[...], preferred_element_type=F32)


def _compress_prompt_kernel(x_ref, pe_ref, w1_ref, b1_ref, w2_ref, o_ref, carry_ref):
    @pl.when(pl.program_id(0) == 0)
    def _():
        carry_ref[...] = jnp.zeros_like(carry_ref)
    o_ref[...] = _compress_rows(x_ref[...], carry_ref, pe_ref, w1_ref, b1_ref, w2_ref).astype(o_ref.dtype)


def _compress_prompt(rows, cw):
    peb, w1b, b1b, w2b = cw
    n = rows.shape[0]
    nc = n // CMP_STRIDE
    wide = CMP_STRIDE * KV_DIM
    x = rows.reshape(nc, wide)
    tc = _row_tile(nc, 256)
    return pl.pallas_call(
        _compress_prompt_kernel,
        grid=(nc // tc,),
        in_specs=[pl.BlockSpec((tc, wide), lambda i: (i, 0)),
                  pl.BlockSpec((2, 1, wide), lambda i: (0, 0, 0)),
                  pl.BlockSpec((2, wide, N_KV * CMP_HIDDEN), lambda i: (0, 0, 0)),
                  pl.BlockSpec((1, N_KV * CMP_HIDDEN), lambda i: (0, 0)),
                  pl.BlockSpec((N_KV * CMP_HIDDEN, KV_DIM), lambda i: (0, 0))],
        out_specs=pl.BlockSpec((tc, KV_DIM), lambda i: (i, 0)),
        out_shape=jax.ShapeDtypeStruct((nc, KV_DIM), BF16),
        scratch_shapes=[pltpu.VMEM((8, N_KV * CMP_HIDDEN), F32)],
        compiler_params=_cparams("arbitrary"),
        name="nsa_compress",
    )(x, peb, w1b, b1b, w2b)


def _masked_softmax_rows(s, valid):
    s = jnp.where(valid, s, NEG_BIG)
    m = jnp.max(s, -1, keepdims=True)
    e = jnp.where(valid, jnp.exp(s - m), 0.0)
    den = jnp.sum(e, -1, keepdims=True)
    return e / jnp.where(den > 0, den, 1.0)


def _select_blocks(imp, cur, n_sel):
    ns = imp.shape[1]
    jj = lax.broadcasted_iota(jnp.int32, (1, ns), 1)
    forced = (jj == 0) | ((jj <= cur) & (jj >= cur - 1))
    v = jnp.where(jj > cur, -FORCE, jnp.where(forced, FORCE, imp))
    sel = jnp.zeros(imp.shape, jnp.bool_)
    for _ in range(min(n_sel, ns)):
        m = jnp.max(v, -1, keepdims=True)
        idx = jnp.min(jnp.where(v == m, jj, ns), -1, keepdims=True)
        hit = jj == idx
        sel = sel | hit
        v = jnp.where(hit, -jnp.inf, v)
    return sel


def _nsa_attn_prompt_kernel(q_ref, gate_ref, ks_ref, vs_ref, *rest):
    kw_refs, vw_refs = rest[0:5], rest[5:10]
    kc_ref, vc_ref, c2s_ref, eg_ref, o_ref = rest[10:15]
    i = pl.program_id(0)
    nh = HEADS_PER_KV
    nc = kc_ref.shape[0]
    ns = c2s_ref.shape[1]
    q = q_ref[...]
    qh = jnp.concatenate([q[:, KV_DIM * h:KV_DIM * (h + 1)] for h in range(nh)], axis=0)
    lane_g = lax.broadcasted_iota(jnp.int32, (1, KV_DIM), 1) // HEAD_DIM
    rows = nh * TQ
    qpos_r = i * TQ + lax.broadcasted_iota(jnp.int32, (rows, 1), 0) % TQ
    qpos_q = i * TQ + lax.broadcasted_iota(jnp.int32, (TQ, 1), 0)
    nt = (((i + 1) * TQ + TK - 1) // TK)
    gate = gate_ref[...]
    hp = lax.Precision.HIGHEST
    nt_dims = (((1,), (1,)), ((), ()))

    kwin = jnp.concatenate([r[...] for r in kw_refs], axis=0)
    vwin = jnp.concatenate([r[...] for r in vw_refs], axis=0)
    kw_pos = i * TQ - WINDOW + lax.broadcasted_iota(jnp.int32, (1, WINDOW + TQ), 1)
    dlt = qpos_r - kw_pos
    win_ok = (dlt >= 0) & (dlt < WINDOW) & (kw_pos >= 0)

    n_idx = lax.broadcasted_iota(jnp.int32, (1, nc), 1)
    cmp_ok = (n_idx >= 1) & (n_idx * CMP_STRIDE + (CMP_STRIDE - 1) <= qpos_r)
    j_row = lax.broadcasted_iota(jnp.int32, (ns, TK), 0)
    j_of_lane = lax.broadcasted_iota(jnp.int32, (ns, TK), 1) // SEL_BLOCK
    key_lane = lax.broadcasted_iota(jnp.int32, (1, TK), 1)

    total = jnp.zeros((rows, KV_DIM), F32)
    for g in range(N_KV):
        qg = jnp.where(lane_g == g, qh, jnp.zeros_like(qh))
        s_c = lax.dot_general(qg, kc_ref[...], nt_dims, preferred_element_type=F32)
        p_c = _masked_softmax_rows(s_c, cmp_ok)
        o_c = jnp.dot(p_c.astype(BF16), vc_ref[...], preferred_element_type=F32)
        pcs = p_c[0:TQ]
        for h in range(1, nh):
            pcs = pcs + p_c[h * TQ:(h + 1) * TQ]
        imp = jnp.dot(pcs, c2s_ref[...], precision=hp, preferred_element_type=F32)
        selb = _select_blocks(imp, qpos_q // SEL_BLOCK, N_SEL).astype(BF16)

        def body(kt, carry):
            m, l, acc = carry
            k0 = pl.multiple_of(kt * TK, TK)
            kblk = ks_ref[pl.ds(k0, TK), :]
            vblk = vs_ref[pl.ds(k0, TK), :]
            s = lax.dot_general(qg, kblk, nt_dims, preferred_element_type=F32)
            onehot = (j_row == kt * (TK // SEL_BLOCK) + j_of_lane).astype(BF16)
            picked = jnp.dot(selb, onehot, preferred_element_type=F32)
            ok = (picked > 0.5) & (k0 + key_lane <= qpos_q)
            bias = jnp.where(ok, 0.0, NEG_BIG)
            s = s + jnp.concatenate([bias] * nh, axis=0)
            m_new = jnp.maximum(m, jnp.max(s, -1, keepdims=True))
            alpha = jnp.exp(m - m_new)
            p = jnp.exp(s - m_new)
            l = alpha * l + jnp.sum(p, -1, keepdims=True)
            acc = alpha * acc + jnp.dot(p.astype(BF16), vblk, preferred_element_type=F32)
            return m_new, l, acc

        init = (jnp.full((rows, 1), NEG_BIG, F32), jnp.zeros((rows, 1), F32), jnp.zeros((rows, KV_DIM), F32))
        _, l_s, acc_s = lax.fori_loop(0, nt, body, init)
        o_s = acc_s / l_s

        s_w = lax.dot_general(qg, kwin, nt_dims, preferred_element_type=F32)
        p_w = _masked_softmax_rows(s_w, win_ok)
        o_w = jnp.dot(p_w.astype(BF16), vwin, preferred_element_type=F32)

        gx = []
        for br in range(3):
            gx.append(jnp.concatenate(
                [jnp.dot(gate, eg_ref[h * 3 + br], precision=hp, preferred_element_type=F32) for h in range(nh)], axis=0))
        og = gx[0] * o_c + gx[1] * o_s + gx[2] * o_w
        total = jnp.where(lane_g == g, og, total)
    for h in range(nh):
        o_ref[:, KV_DIM * h:KV_DIM * (h + 1)] = total[h * TQ:(h + 1) * TQ].astype(o_ref.dtype)


def _cmp_to_sel(nc, ns):
    blk = jnp.arange(nc)[:, None] - 1
    i = blk * CMP_STRIDE
    j = jnp.arange(ns)[None, :] * SEL_BLOCK
    return ((blk >= 0) & (i < j + SEL_BLOCK) & (i + CMP_LEN > j)).astype(F32)


def _gate_expand():
    c = jnp.arange(128)[None, :, None]
    h = jnp.arange(HEADS_PER_KV)[:, None, None, None]
    br = jnp.arange(3)[None, :, None, None]
    g = (jnp.arange(KV_DIM) // HEAD_DIM)[None, None, None, :]
    eg = (c[None] == (g * HEADS_PER_KV + h) * 3 + br).astype(F32)
    return eg.reshape(HEADS_PER_KV * 3, 128, KV_DIM)


def _nsa_attn_prompt(q, gates, kvb, kc, vc):
    n = q.shape[0]
    nc, ns = n // CMP_STRIDE, n // SEL_BLOCK
    assert n % TK == 0 and WINDOW % TQ == 0
    wpad = jnp.pad(kvb[:, 2 * KV_DIM:], ((WINDOW, 0), (0, 0)))
    c2s = _cmp_to_sel(nc, ns)
    eg = _gate_expand()
    nwb = WINDOW // TQ + 1
    const2 = lambda i: (0, 0)
    one = pl.Buffered(1)
    in_specs = [
        pl.BlockSpec((TQ, D_MODEL), lambda i: (i, 0)),
        pl.BlockSpec((TQ, 128), lambda i: (i, 0)),
        pl.BlockSpec((n, KV_DIM), lambda i: (0, 0), pipeline_mode=one),
        pl.BlockSpec((n, KV_DIM), lambda i: (0, 1), pipeline_mode=one),
    ]
    in_specs += [pl.BlockSpec((TQ, KV_DIM), functools.partial(lambda i, k: (i + k, 0), k=k)) for k in range(nwb)]
    in_specs += [pl.BlockSpec((TQ, KV_DIM), functools.partial(lambda i, k: (i + k, 1), k=k)) for k in range(nwb)]
    in_specs += [
        pl.BlockSpec((nc, KV_DIM), const2, pipeline_mode=one),
        pl.BlockSpec((nc, KV_DIM), const2, pipeline_mode=one),
        pl.BlockSpec((nc, ns), const2, pipeline_mode=one),
        pl.BlockSpec((HEADS_PER_KV * 3, 128, KV_DIM), lambda i: (0, 0, 0), pipeline_mode=one),
    ]
    return pl.pallas_call(
        _nsa_attn_prompt_kernel,
        grid=(n // TQ,),
        in_specs=in_specs,
        out_specs=pl.BlockSpec((TQ, D_MODEL), lambda i: (i, 0)),
        out_shape=jax.ShapeDtypeStruct((n, D_MODEL), BF16),
        compiler_params=_cparams("parallel"),
        name="nsa_attn_prompt",
    )(q, gates, kvb, kvb, *([wpad] * (2 * nwb)), kc, vc, c2s, eg)


def _out_proj_ln_kernel(x_ref, o_ref, w_ref, g_ref, b_ref, y_ref):
    y = jnp.dot(o_ref[...], w_ref[...], preferred_element_type=F32)
    y_ref[...] = _layer_norm(ALPHA * x_ref[...] + y, g_ref[...], b_ref[...])


def _out_proj_ln(x, o, w_o, g, b):
    n = x.shape[0]
    tm = _row_tile(n, 512)
    const = lambda i: (0, 0)
    return pl.pallas_call(
        _out_proj_ln_kernel,
        grid=(n // tm,),
        in_specs=[pl.BlockSpec((tm, D_MODEL), lambda i: (i, 0)), pl.BlockSpec((tm, D_MODEL), lambda i: (i, 0)),
                  pl.BlockSpec((D_MODEL, D_MODEL), const), pl.BlockSpec((1, D_MODEL), const),
                  pl.BlockSpec((1, D_MODEL), const)],
        out_specs=pl.BlockSpec((tm, D_MODEL), lambda i: (i, 0)),
        out_shape=jax.ShapeDtypeStruct((n, D_MODEL), F32),
        compiler_params=_cparams("parallel"),
        name="nsa_out_ln",
    )(x, o, w_o[_perm_heads(), :].astype(BF16), g.reshape(1, -1), b.reshape(1, -1))


def _nsa_prompt_layer(x, w_in, pe, w1, b1, w2, w_o, g, b):
    q, k_c, v_c, k_s, v_s, k_w, v_w, kvb, gates = _nsa_proj(x, w_in)
    kc = _compress_prompt(k_c, _cmp_weights(pe[0], w1[0], b1[0], w2[0]))
    vc = _compress_prompt(v_c, _cmp_weights(pe[1], w1[1], b1[1], w2[1]))
    o = _nsa_attn_prompt(q, gates, kvb, kc, vc)
    y = _out_proj_ln(x, o, w_o, g, b)
    return y, (k_c, v_c, k_s, v_s, k_w, v_w)


def _compress_sample_kernel(pt_ref, *refs, n_pg):
    pages = refs[:n_pg]
    new_ref, pe_ref, w1_ref, b1_ref, w2_ref, o_ref, carry_ref = refs[n_pg:]
    s = pl.program_id(1)
    last = pl.num_programs(1) - 1

    @pl.when(s == 0)
    def _():
        carry_ref[...] = jnp.zeros_like(carry_ref)
    first = jnp.where(s == last, new_ref[0], pages[0][0])
    x = jnp.concatenate([first] + [p[0] for p in pages[1:]], axis=0)
    o_ref[0] = _compress_rows(x, carry_ref, pe_ref, w1_ref, b1_ref, w2_ref).astype(o_ref.dtype)


def _compress_sample(pool, page_table, new_rows, cw):
    peb, w1b, b1b, w2b = cw
    B, n_pages = page_table.shape
    t = new_rows.shape[1]
    assert t <= CMP_STRIDE
    wide = CMP_STRIDE * KV_DIM
    cpp = PAGE_SIZE // CMP_STRIDE
    n_pg = min(16, n_pages)
    assert n_pages % n_pg == 0
    steps = n_pages // n_pg + 1
    pool2 = pool.reshape(pool.shape[0], cpp, wide)
    newc = jnp.pad(new_rows.reshape(B, 1, t * KV_DIM), ((0, 0), (0, cpp - 1), (0, wide - t * KV_DIM)))

    def page_map(b, s, pt, k):
        return (pt[b * n_pages + jnp.minimum(s * n_pg + k, n_pages - 1)], 0, 0)

    tc = n_pg * cpp
    grid_spec = pltpu.PrefetchScalarGridSpec(
        num_scalar_prefetch=1,
        grid=(B, steps),
        in_specs=[pl.BlockSpec((1, cpp, wide), functools.partial(page_map, k=k)) for k in range(n_pg)] + [
            pl.BlockSpec((1, cpp, wide), lambda b, s, pt: (b, 0, 0)),
            pl.BlockSpec((2, 1, wide), lambda b, s, pt: (0, 0, 0)),
            pl.BlockSpec((2, wide, N_KV * CMP_HIDDEN), lambda b, s, pt: (0, 0, 0)),
            pl.BlockSpec((1, N_KV * CMP_HIDDEN), lambda b, s, pt: (0, 0)),
            pl.BlockSpec((N_KV * CMP_HIDDEN, KV_DIM), lambda b, s, pt: (0, 0))],
        out_specs=pl.BlockSpec((1, tc, KV_DIM), lambda b, s, pt: (b, s, 0)),
        scratch_shapes=[pltpu.VMEM((8, N_KV * CMP_HIDDEN), F32)],
    )
    return pl.pallas_call(
        functools.partial(_compress_sample_kernel, n_pg=n_pg),
        grid_spec=grid_spec,
        out_shape=jax.ShapeDtypeStruct((B, steps * tc, KV_DIM), BF16),
        compiler_params=_cparams("parallel", "arbitrary"),
        name="nsa_compress_paged",
    )(page_table.reshape(-1), *([pool2] * n_pg), newc, peb, w1b, b1b, w2b)


def _nsa_attn_sample_kernel(pt_ref, q_ref, gate_ref, new_ref, *refs, n_pg, past, t_len, n_blk):
    kpages, vpages = refs[:n_pg], refs[n_pg:2 * n_pg]
    kwb_ref, vwb_ref, kc_ref, vc_ref, c2s_ref, eg_ref, o_ref, qs_ref, selb_ref, m_ref, l_ref, acc_ref, oo_ref = refs[2 * n_pg:]
    s = pl.program_id(1)
    nh = HEADS_PER_KV
    rows = nh * N_KV * t_len
    nc = kc_ref.shape[1]
    nsp = c2s_ref.shape[1]
    hp = lax.Precision.HIGHEST
    nt_dims = (((1,), (1,)), ((), ()))
    r_iota = lax.broadcasted_iota(jnp.int32, (rows, 1), 0)
    qpos_r = past + r_iota % t_len
    lane_g = lax.broadcasted_iota(jnp.int32, (1, KV_DIM), 1) // HEAD_DIM
    row_g = (r_iota // t_len) % N_KV

    def tile_rows(x, k):
        return jnp.concatenate([x] * k, axis=0)

    @pl.when(s == 0)
    def _():
        qf = q_ref[...].astype(F32)
        q128 = jnp.concatenate([tile_rows(qf[:, KV_DIM * h:KV_DIM * (h + 1)], N_KV) for h in range(nh)], axis=0)
        q128 = jnp.where(lane_g == row_g, q128, 0.0).astype(BF16)
        qs_ref[...] = q128
        gate = gate_ref[...]
        gx = [jnp.concatenate([tile_rows(jnp.dot(gate, eg_ref[h * 3 + br], precision=hp, preferred_element_type=F32), N_KV)
                               for h in range(nh)], axis=0) for br in range(3)]
        new = new_ref[...]
        n_idx = lax.broadcasted_iota(jnp.int32, (1, nc), 1)
        cmp_ok = (n_idx >= 1) & (n_idx <= n_blk) & (n_idx * CMP_STRIDE + (CMP_STRIDE - 1) <= qpos_r)
        s_c = lax.dot_general(q128, kc_ref[0], nt_dims, preferred_element_type=F32)
        p_c = _masked_softmax_rows(s_c, cmp_ok)
        o_c = jnp.dot(p_c.astype(BF16), vc_ref[0], preferred_element_type=F32)
        gq = N_KV * t_len
        pcs = p_c[0:gq]
        for h in range(1, nh):
            pcs = pcs + p_c[h * gq:(h + 1) * gq]
        imp = jnp.dot(pcs, c2s_ref[...], precision=hp, preferred_element_type=F32)
        cur = (past + lax.broadcasted_iota(jnp.int32, (gq, 1), 0) % t_len) // SEL_BLOCK
        sel = _select_blocks(imp, cur, N_SEL).astype(F32)
        selb_ref[...] = tile_rows(sel, nh).astype(BF16)
        kwin = jnp.concatenate([kwb_ref[0], new[:, 2 * KV_DIM:3 * KV_DIM]], axis=0).astype(BF16)
        vwin = jnp.concatenate([vwb_ref[0], new[:, 3 * KV_DIM:]], axis=0).astype(BF16)
        wb = kwb_ref.shape[1]
        kw_pos = past - wb + lax.broadcasted_iota(jnp.int32, (1, wb + t_len), 1)
        dlt = qpos_r - kw_pos
        win_ok = (dlt >= 0) & (dlt < WINDOW) & (kw_pos >= 0)
        s_w = lax.dot_general(q128, kwin, nt_dims, preferred_element_type=F32)
        p_w = _masked_softmax_rows(s_w, win_ok)
        o_w = jnp.dot(p_w.astype(BF16), vwin, preferred_element_type=F32)
        oo_ref[0] = gx[0] * o_c + gx[2] * o_w
        oo_ref[1] = gx[1]
        knew = new[:, 0:KV_DIM].astype(BF16)
        vnew = new[:, KV_DIM:2 * KV_DIM].astype(BF16)
        s_n = lax.dot_general(q128, knew, nt_dims, preferred_element_type=F32)
        jn = lax.broadcasted_iota(jnp.int32, (nsp, t_len), 0)
        picked = jnp.dot(selb_ref[...], (jn == past // SEL_BLOCK).astype(BF16), preferred_element_type=F32)
        ok = (picked > 0.5) & (past + lax.broadcasted_iota(jnp.int32, (1, t_len), 1) <= qpos_r)
        s_n = jnp.where(ok, s_n, NEG_BIG)
        m0 = jnp.max(s_n, -1, keepdims=True)
        p0 = jnp.where(ok, jnp.exp(s_n - m0), 0.0)
        m_ref[...] = m0
        l_ref[...] = jnp.sum(p0, -1, keepdims=True)
        acc_ref[...] = jnp.dot(p0.astype(BF16), vnew, preferred_element_type=F32)

    tk = n_pg * PAGE_SIZE
    kblk = jnp.concatenate([p[0] for p in kpages], axis=0).astype(BF16)
    vblk = jnp.concatenate([p[0] for p in vpages], axis=0).astype(BF16)
    sc = lax.dot_general(qs_ref[...], kblk, nt_dims, preferred_element_type=F32)
    j_row = lax.broadcasted_iota(jnp.int32, (nsp, tk), 0)
    j_of_lane = lax.broadcasted_iota(jnp.int32, (nsp, tk), 1) // SEL_BLOCK
    onehot = (j_row == s * (tk // SEL_BLOCK) + j_of_lane).astype(BF16)
    picked = jnp.dot(selb_ref[...], onehot, preferred_element_type=F32)
    sc = jnp.where(picked > 0.5, sc, NEG_BIG)
    m = m_ref[...]
    m_new = jnp.maximum(m, jnp.max(sc, -1, keepdims=True))
    alpha = jnp.exp(m - m_new)
    p = jnp.where(picked > 0.5, jnp.exp(sc - m_new), 0.0)
    l_ref[...] = alpha * l_ref[...] + jnp.sum(p, -1, keepdims=True)
    acc_ref[...] = alpha * acc_ref[...] + jnp.dot(p.astype(BF16), vblk, preferred_element_type=F32)
    m_ref[...] = m_new

    @pl.when(s == pl.num_programs(1) - 1)
    def _():
        l = l_ref[...]
        o_s = acc_ref[...] / jnp.where(l > 0, l, 1.0)
        total = oo_ref[0] + oo_ref[1] * o_s
        gq = N_KV * t_len
        for h in range(nh):
            oh = jnp.zeros((t_len, KV_DIM), F32)
            for g in range(N_KV):
                oh = jnp.where(lane_g == g, total[h * gq + g * t_len:h * gq + (g + 1) * t_len], oh)
            o_ref[:, KV_DIM * h:KV_DIM * (h + 1)] = oh.astype(o_ref.dtype)


def _nsa_attn_sample(q, gates, new, page_table, pool_ks, pool_vs, buf_kw, buf_vw, kc, vc, *, t_len):
    B, n_pages = page_table.shape
    past = n_pages * PAGE_SIZE
    total = past + t_len
    n_blk = -(-total // CMP_STRIDE) - 1
    ns = -(-total // SEL_BLOCK)
    nsp = -(-ns // 128) * 128
    nc = kc.shape[1]
    c2s = jnp.pad(_cmp_to_sel(nc, ns), ((0, 0), (0, nsp - ns)))
    eg = _gate_expand()
    n_pg = min(8, n_pages)
    assert n_pages % n_pg == 0 and t_len % 8 == 0
    steps = n_pages // n_pg
    rows = N_HEADS * t_len
    wb = buf_kw.shape[1]
    one = pl.Buffered(1)

    def page_map(b, s, pt, k):
        return (pt[b * n_pages + s * n_pg + k], 0, 0)

    per_b = lambda w: pl.BlockSpec((t_len, w), lambda b, s, pt: (b, 0))
    grid_spec = pltpu.PrefetchScalarGridSpec(
        num_scalar_prefetch=1,
        grid=(B, steps),
        in_specs=[per_b(D_MODEL), per_b(128), per_b(4 * KV_DIM)]
        + [pl.BlockSpec((1, PAGE_SIZE, KV_DIM), functools.partial(page_map, k=k)) for k in range(n_pg)] * 2
        + [pl.BlockSpec((1, wb, KV_DIM), lambda b, s, pt: (b, 0, 0)),
           pl.BlockSpec((1, wb, KV_DIM), lambda b, s, pt: (b, 0, 0)),
           pl.BlockSpec((1, nc, KV_DIM), lambda b, s, pt: (b, 0, 0)),
           pl.BlockSpec((1, nc, KV_DIM), lambda b, s, pt: (b, 0, 0)),
           pl.BlockSpec((nc, nsp), lambda b, s, pt: (0, 0), pipeline_mode=one),
           pl.BlockSpec((HEADS_PER_KV * 3, 128, KV_DIM), lambda b, s, pt: (0, 0, 0), pipeline_mode=one)],
        out_specs=pl.BlockSpec((t_len, D_MODEL), lambda b, s, pt: (b, 0)),
        scratch_shapes=[pltpu.VMEM((rows, KV_DIM), BF16), pltpu.VMEM((rows, nsp), BF16),
                        pltpu.VMEM((rows, 1), F32), pltpu.VMEM((rows, 1), F32), pltpu.VMEM((rows, KV_DIM), F32),
                        pltpu.VMEM((2, rows, KV_DIM), F32)],
    )
    return pl.pallas_call(
        functools.partial(_nsa_attn_sample_kernel, n_pg=n_pg, past=past, t_len=t_len, n_blk=n_blk),
        grid_spec=grid_spec,
        out_shape=jax.ShapeDtypeStruct((B * t_len, D_MODEL), BF16),
        compiler_params=_cparams("parallel", "arbitrary"),
        name="nsa_attn_sample",
    )(page_table.reshape(-1), q, gates, new, *([pool_ks] * n_pg), *([pool_vs] * n_pg), buf_kw, buf_vw, kc, vc, c2s, eg)


def _nsa_sample_layer(x, page_table, pool_kc, pool_vc, pool_ks, pool_vs, buf_kw, buf_vw,
                      w_in, pe, w1, b1, w2, w_o, g, b, *, t_len):
    B = page_table.shape[0]
    q, k_c, v_c, k_s, v_s, k_w, v_w, _, gates = _nsa_proj(x, w_in)
    flat = lambda p: p.reshape(p.shape[0], p.shape[1], KV_DIM)
    kc = _compress_sample(flat(pool_kc), page_table, k_c.reshape(B, t_len, KV_DIM), _cmp_weights(pe[0], w1[0], b1[0], w2[0]))
    vc = _compress_sample(flat(pool_vc), page_table, v_c.reshape(B, t_len, KV_DIM), _cmp_weights(pe[1], w1[1], b1[1], w2[1]))
    new = jnp.concatenate([k_s, v_s, k_w, v_w], axis=1)
    o = _nsa_attn_sample(q, gates, new, page_table, flat(pool_ks), flat(pool_vs), flat(buf_kw), flat(buf_vw),
                         kc, vc, t_len=t_len)
    y = _out_proj_ln(x, o, w_o, g, b)
    return y, (k_c, v_c, k_s, v_s, k_w, v_w), o


def kernel(x_prompt, x_sample, state_s5_re, state_s5_im, cache_k_cmp, cache_v_cmp, cache_k_sel, cache_v_sel,
           cache_k_win, cache_v_win, state_ffn_conv, page_table,
           s5_a_re, s5_a_im, s5_log_dt, s5_b_re, s5_b_im, s5_c_re, s5_c_im, s5_d, s5_w_glu,
           nsa_w_in, nsa_cmp_pe, nsa_cmp_w1, nsa_cmp_b1, nsa_cmp_w2, nsa_w_o,
           ffn_w_up, ffn_conv_w, ffn_conv_b, ffn_w_down,
           ln_mix_g, ln_mix_b, ln_ffn_g, ln_ffn_b):
    assert s5_a_re.shape[0] == 1 and nsa_w_in.shape[0] == 1 and ffn_w_up.shape[0] == DEPTH
    bsz, L, _ = x_prompt.shape
    B, t_len, _ = x_sample.shape
    xp, xs, st_p, st_s = _s5_layer(x_prompt, x_sample, state_s5_re[0], state_s5_im[0], s5_a_re[0], s5_a_im[0],
                                   s5_log_dt[0], s5_b_re[0], s5_b_im[0], s5_c_re[0], s5_c_im[0], s5_d[0],
                                   s5_w_glu[0], ln_mix_g[0], ln_mix_b[0])
    fw = (ffn_w_up[0], ffn_conv_w[0], ffn_conv_b[0], ffn_w_down[0], ln_ffn_g[0], ln_ffn_b[0])
    xp, cp0 = _ffn(xp, None, *fw, seq_len=L)
    xs, cs0 = _ffn(xs, state_ffn_conv[0], *fw, seq_len=t_len)
    nw = (nsa_w_in[0], nsa_cmp_pe[0], nsa_cmp_w1[0], nsa_cmp_b1[0], nsa_cmp_w2[0], nsa_w_o[0],
          ln_mix_g[1], ln_mix_b[1])
    xp, rows_p = _nsa_prompt_layer(xp, *nw)
    xs, rows_s, _ = _nsa_sample_layer(xs, page_table, cache_k_cmp[0], cache_v_cmp[0], cache_k_sel[0], cache_v_sel[0],
                                      cache_k_win[0], cache_v_win[0], *nw, t_len=t_len)
    fw = (ffn_w_up[1], ffn_conv_w[1], ffn_conv_b[1], ffn_w_down[1], ln_ffn_g[1], ln_ffn_b[1])
    xp, cp1 = _ffn(xp, None, *fw, seq_len=L)
    xs, cs1 = _ffn(xs, state_ffn_conv[1], *fw, seq_len=t_len)

    heads = lambda r, lead: r.reshape(1, lead, -1, N_KV, HEAD_DIM)
    win = min(WINDOW, L)
    wb = cache_k_win.shape[2]
    keep = min(WINDOW, wb + t_len)
    win_s = lambda buf, new: jnp.concatenate([buf[0], new.reshape(B, t_len, N_KV, HEAD_DIM)], axis=1)[None, :, wb + t_len - keep:]
    return (xp[None], xs.reshape(B, t_len, D_MODEL),
            st_p[0][None], st_p[1][None], st_s[0][None], st_s[1][None],
            heads(rows_p[0], 1), heads(rows_p[1], 1), heads(rows_p[2], 1), heads(rows_p[3], 1),
            heads(rows_s[0], B), heads(rows_s[1], B), heads(rows_s[2], B), heads(rows_s[3], B),
            heads(rows_p[4][L - win:], 1), heads(rows_p[5][L - win:], 1),
            win_s(cache_k_win, rows_s[4]), win_s(cache_v_win, rows_s[5]),
            jnp.stack([cp0, cp1]), jnp.stack([cs0, cs1]))


def _s5_layer(xp, xs, h_re, h_im, a_re, a_im, log_dt, b_re, b_im, c_re, c_im, d_skip, w_glu, g, b):
    assert xp.shape[0] == 1
    mats = _s5_prep(a_re, a_im, log_dt, b_re, b_im, c_re, c_im)
    P = S5_STATE
    x2p = xp[0]
    yp, hfp = _s5_core_prompt(x2p, mats)
    op = _s5_glu_ln(x2p, yp, d_skip, w_glu, g, b)
    x2s = xs.reshape(-1, D_MODEL)
    ys, hfs = _s5_core_sample(xs, h_re, h_im, mats)
    os_ = _s5_glu_ln(x2s, ys, d_skip, w_glu, g, b)
    st_p = (hfp[None, :, :P], hfp[None, :, P:])
    hfs = hfs.transpose(1, 0, 2)
    st_s = (hfs[:, :, :P], hfs[:, :, P:])
    return op, os_, st_p, st_s
```

```python
import functools
import math

import jax
import jax.numpy as jnp
from jax import lax
from jax.experimental import pallas as pl
from jax.experimental.pallas import tpu as pltpu

F32 = jnp.float32
BF16 = jnp.bfloat16

D_MODEL = 1024
S5_GROUP = 16
S5_GROUPS = D_MODEL // S5_GROUP
S5_STATE = 64
S5_T = 16
N_HEADS = 16
HEAD_DIM = 64
N_KV = 4
HEADS_PER_KV = N_HEADS // N_KV
KV_DIM = N_KV * HEAD_DIM
CMP_LEN = 32
CMP_STRIDE = 16
CMP_HIDDEN = 2 * HEAD_DIM
SEL_BLOCK = 64
N_SEL = 16
WINDOW = 512
PAGE_SIZE = 128
FORCE = 1e6
D_FF = 2816
CONV_W = 3
FFN_CHUNK = 256
DEPTH = 2
ALPHA = (2.0 * DEPTH) ** 0.25
LN_EPS = 1e-5
NEG_BIG = -1e30

VMEM_LIMIT = 56 * 1024 * 1024


def _cparams(*sem):
    return pltpu.CompilerParams(dimension_semantics=sem, vmem_limit_bytes=VMEM_LIMIT)


def _layer_norm(r, g, b):
    mu = jnp.mean(r, -1, keepdims=True)
    c = r - mu
    var = jnp.mean(c * c, -1, keepdims=True)
    return c * lax.rsqrt(var + LN_EPS) * g + b


def _swap(x):
    return pltpu.roll(x, S5_STATE, axis=x.ndim - 1)


def _lo_mask(shape):
    return lax.broadcasted_iota(jnp.int32, shape, len(shape) - 1) < S5_STATE


def _cmul(x, w):
    ws = _swap(w)
    lo = _lo_mask(w.shape)
    wr = jnp.where(lo, w, ws)
    wi = jnp.where(lo, -ws, w)
    return x * wr + _swap(x) * wi


def _s5_prep_kernel(a_ref, dt_ref, bt_ref, btt_ref, ct_ref, m_ref, w_ref, v_ref, z_ref):
    P = S5_STATE
    T = S5_T
    a = a_ref[0]
    dt = jnp.exp(dt_ref[0])
    lo = _lo_mask(a.shape)
    a_sw = _swap(a)
    are = jnp.where(lo, a, a_sw)
    aim = jnp.where(lo, a_sw, a)
    mag = jnp.exp(are * dt)
    ang = aim * dt
    ab = mag * jnp.where(lo, jnp.cos(ang), jnp.sin(ang))
    ab_sw = _swap(ab)
    abr = jnp.where(lo, ab, ab_sw)
    abi = jnp.where(lo, ab_sw, ab)
    den = are * are + aim * aim
    nr = abr - 1.0
    f = jnp.where(lo, nr * are + abi * aim, abi * are - nr * aim) / den

    pw = [ab]
    for _ in range(4):
        pw.append(_cmul(pw[-1], pw[-1]))
    z_ref[0] = jnp.concatenate(
        [jnp.broadcast_to(pw[3], (4, 2 * P)), jnp.broadcast_to(pw[4], (4, 2 * P))], axis=0)

    rows = lax.broadcasted_iota(jnp.int32, (T * S5_GROUP, 2 * P), 0) // S5_GROUP
    e = (T - 1) - rows
    one = jnp.where(_lo_mask((T * S5_GROUP, 2 * P)), 1.0, 0.0).astype(F32)
    apow = one
    for b in range(4):
        apow = jnp.where(((e >> b) & 1) == 1, _cmul(apow, pw[b]), apow)
    af = _cmul(apow, f)
    w_ref[0] = _cmul(af, btt_ref[0])

    col = [jnp.transpose(jnp.broadcast_to(p, (2 * P, 2 * P))) for p in pw[:4]]
    lane_e = lax.broadcasted_iota(jnp.int32, (P, T * S5_GROUP), 1) // S5_GROUP

    def col_cmul(xr, xi, c):
        cr = jnp.concatenate([c[:P], c[:P]], axis=1)
        ci = jnp.concatenate([c[P:], c[P:]], axis=1)
        return xr * cr - xi * ci, xr * ci + xi * cr

    pr = jnp.ones((P, T * S5_GROUP), F32)
    pi = jnp.zeros((P, T * S5_GROUP), F32)
    for b in range(4):
        nr_, ni_ = col_cmul(pr, pi, col[b])
        sel = ((lane_e >> b) & 1) == 1
        pr = jnp.where(sel, nr_, pr)
        pi = jnp.where(sel, ni_, pi)
    ctr = ct_ref[0, 0]
    cti = ct_ref[0, 1]
    qr = pr * ctr - pi * cti
    qi = pr * cti + pi * ctr
    q1r, q1i = col_cmul(qr, qi, col[0])
    v_ref[0] = jnp.concatenate([q1r, -q1i], axis=0).astype(v_ref.dtype)

    fr = jnp.where(lo, f, _swap(f))[:, :P]
    fi = jnp.where(lo, _swap(f), f)[:, :P]
    btr = bt_ref[0, 0]
    bti = bt_ref[0, 1]
    bfr = btr * fr - bti * fi
    bfi = btr * fi + bti * fr
    hp = lax.Precision.HIGHEST
    kk = (jnp.dot(bfr, qr, precision=hp, preferred_element_type=F32)
          - jnp.dot(bfi, qi, precision=hp, preferred_element_type=F32))
    lane = lax.broadcasted_iota(jnp.int32, kk.shape, 1)
    for s in range(T):
        blk = kk if s == 0 else jnp.where(lane >= S5_GROUP * s, pltpu.roll(kk, S5_GROUP * s, axis=1), 0.0)
        m_ref[0, S5_GROUP * s:S5_GROUP * (s + 1), :] = blk.astype(m_ref.dtype)


def _s5_prep(a_re, a_im, log_dt, b_re, b_im, c_re, c_im):
    G, P, T = S5_GROUPS, S5_STATE, S5_T
    a_pk = jnp.concatenate([a_re, a_im], axis=-1).reshape(G, 1, 2 * P)
    dt_pk = jnp.broadcast_to(log_dt.reshape(G, 1, 1), (G, 1, 2 * P))
    bt = jnp.stack([b_re, b_im], axis=1).transpose(0, 1, 3, 2)
    btt = jnp.tile(jnp.concatenate([b_re, b_im], axis=1).transpose(0, 2, 1), (1, T, 1))
    ct = jnp.tile(jnp.stack([c_re, c_im], axis=1).transpose(0, 1, 3, 2), (1, 1, 1, T))
    n = T * S5_GROUP
    return pl.pallas_call(
        _s5_prep_kernel,
        grid=(G,),
        in_specs=[
            pl.BlockSpec((1, 1, 2 * P), lambda g: (g, 0, 0)),
            pl.BlockSpec((1, 1, 2 * P), lambda g: (g, 0, 0)),
            pl.BlockSpec((1, 2, S5_GROUP, P), lambda g: (g, 0, 0, 0)),
            pl.BlockSpec((1, n, 2 * P), lambda g: (g, 0, 0)),
            pl.BlockSpec((1, 2, P, n), lambda g: (g, 0, 0, 0)),
        ],
        out_specs=[
            pl.BlockSpec((1, n, n), lambda g: (g, 0, 0)),
            pl.BlockSpec((1, n, 2 * P), lambda g: (g, 0, 0)),
            pl.BlockSpec((1, 2 * P, n), lambda g: (g, 0, 0)),
            pl.BlockSpec((1, 8, 2 * P), lambda g: (g, 0, 0)),
        ],
        out_shape=[
            jax.ShapeDtypeStruct((G, n, n), BF16),
            jax.ShapeDtypeStruct((G, n, 2 * P), F32),
            jax.ShapeDtypeStruct((G, 2 * P, n), BF16),
            jax.ShapeDtypeStruct((G, 8, 2 * P), F32),
        ],
        compiler_params=_cparams("parallel"),
        name="s5_prep",
    )(a_pk, dt_pk, bt, btt, ct)


def _s5_scan_kernel(u_ref, m_ref, w_ref, v_ref, z_ref, y_ref, hf_ref):
    u = u_ref[0]
    C = u.shape[0]
    s = jnp.dot(u, w_ref[0].astype(BF16), preferred_element_type=F32)
    z = z_ref[0, 4:5, :]
    row = lax.broadcasted_iota(jnp.int32, s.shape, 0)
    h = s
    d = 1
    while d < C:
        hs = jnp.where(row >= d, pltpu.roll(h, d, axis=0), 0.0)
        h = h + _cmul(hs, z)
        z = _cmul(z, z)
        d *= 2
    hprev = jnp.where(row >= 1, pltpu.roll(h, 1, axis=0), 0.0)
    y = (jnp.dot(u, m_ref[0], preferred_element_type=F32)
         + jnp.dot(hprev.astype(BF16), v_ref[0], preferred_element_type=F32))
    y_ref[0] = y.astype(y_ref.dtype)
    hf_ref[0] = h[C - 8:, :]


def _s5_step_kernel(u_ref, h0_ref, m_ref, w_ref, v_ref, z_ref, y_ref, hf_ref, *, t_len):
    n = t_len * S5_GROUP
    off = (S5_T - t_len) * S5_GROUP
    u = u_ref[0]
    h0 = h0_ref[0]
    hp = lax.Precision.HIGHEST
    s = jnp.dot(u, w_ref[0, off:, :], precision=hp, preferred_element_type=F32)
    z = z_ref[0, 0:1, :] if t_len * 2 == S5_T else z_ref[0, 4:5, :]
    hf_ref[0] = _cmul(h0, z) + s
    y = (jnp.dot(u.astype(BF16), m_ref[0, :n, :n], preferred_element_type=F32)
         + jnp.dot(h0.astype(BF16), v_ref[0, :, :n], preferred_element_type=F32))
    y_ref[0] = y.astype(y_ref.dtype)


def _s5_core_prompt(x, mats):
    m, w, v, z = mats
    L = x.shape[0]
    G, T = S5_GROUPS, S5_T
    C = L // T
    n = T * S5_GROUP
    u = x.reshape(C, T, G, S5_GROUP).transpose(2, 0, 1, 3).reshape(G, C, n).astype(BF16)
    y, hf = pl.pallas_call(
        _s5_scan_kernel,
        grid=(G,),
        in_specs=[
            pl.BlockSpec((1, C, n), lambda g: (g, 0, 0)),
            pl.BlockSpec((1, n, n), lambda g: (g, 0, 0)),
            pl.BlockSpec((1, n, 2 * S5_STATE), lambda g: (g, 0, 0)),
            pl.BlockSpec((1, 2 * S5_STATE, n), lambda g: (g, 0, 0)),
            pl.BlockSpec((1, 8, 2 * S5_STATE), lambda g: (g, 0, 0)),
        ],
        out_specs=[
            pl.BlockSpec((1, C, n), lambda g: (g, 0, 0)),
            pl.BlockSpec((1, 8, 2 * S5_STATE), lambda g: (g, 0, 0)),
        ],
        out_shape=[
            jax.ShapeDtypeStruct((G, C, n), BF16),
            jax.ShapeDtypeStruct((G, 8, 2 * S5_STATE), F32),
        ],
        compiler_params=_cparams("parallel"),
        name="s5_scan",
    )(u, m, w, v, z)
    y = y.reshape(G, C, T, S5_GROUP).transpose(1, 2, 0, 3).reshape(L, D_MODEL)
    return y, hf[:, 7, :]


def _s5_core_sample(x, h_re, h_im, mats):
    m, w, v, z = mats
    B, t_len, _ = x.shape
    assert t_len in (S5_T // 2, S5_T)
    G = S5_GROUPS
    n = t_len * S5_GROUP
    u = x.reshape(B, t_len, G, S5_GROUP).transpose(2, 0, 1, 3).reshape(G, B, n)
    h0 = jnp.concatenate([h_re, h_im], axis=-1).transpose(1, 0, 2)
    nf = S5_T * S5_GROUP
    y, hf = pl.pallas_call(
        functools.partial(_s5_step_kernel, t_len=t_len),
        grid=(G,),
        in_specs=[
            pl.BlockSpec((1, B, n), lambda g: (g, 0, 0)),
            pl.BlockSpec((1, B, 2 * S5_STATE), lambda g: (g, 0, 0)),
            pl.BlockSpec((1, nf, nf), lambda g: (g, 0, 0)),
            pl.BlockSpec((1, nf, 2 * S5_STATE), lambda g: (g, 0, 0)),
            pl.BlockSpec((1, 2 * S5_STATE, nf), lambda g: (g, 0, 0)),
            pl.BlockSpec((1, 8, 2 * S5_STATE), lambda g: (g, 0, 0)),
        ],
        out_specs=[
            pl.BlockSpec((1, B, n), lambda g: (g, 0, 0)),
            pl.BlockSpec((1, B, 2 * S5_STATE), lambda g: (g, 0, 0)),
        ],
        out_shape=[
            jax.ShapeDtypeStruct((G, B, n), BF16),
            jax.ShapeDtypeStruct((G, B, 2 * S5_STATE), F32),
        ],
        compiler_params=_cparams("parallel"),
        name="s5_step",
    )(u, h0, m, w, v, z)
    y = y.reshape(G, B, t_len, S5_GROUP).transpose(1, 2, 0, 3).reshape(B * t_len, D_MODEL)
    return y, hf


def _s5_glu_ln_kernel(x_ref, y_ref, d_ref, w_ref, g_ref, b_ref, o_ref):
    x = x_ref[...]
    yy = y_ref[...].astype(F32) + x * d_ref[...]
    zg = jax.nn.gelu(yy).astype(BF16)
    zz = jnp.dot(zg, w_ref[...], preferred_element_type=F32)
    out = zz[:, :D_MODEL] * jax.nn.sigmoid(zz[:, D_MODEL:])
    o_ref[...] = _layer_norm(ALPHA * x + out, g_ref[...], b_ref[...])


def _row_tile(n, cap):
    t = min(n, cap)
    while n % t:
        t //= 2
    return t


def _s5_glu_ln(x, y, d_skip, w_glu, g, b):
    n = x.shape[0]
    tm = _row_tile(n, 512)
    const = lambda i: (0, 0)
    return pl.pallas_call(
        _s5_glu_ln_kernel,
        grid=(n // tm,),
        in_specs=[
            pl.BlockSpec((tm, D_MODEL), lambda i: (i, 0)),
            pl.BlockSpec((tm, D_MODEL), lambda i: (i, 0)),
            pl.BlockSpec((1, D_MODEL), const),
            pl.BlockSpec((D_MODEL, 2 * D_MODEL), const),
            pl.BlockSpec((1, D_MODEL), const),
            pl.BlockSpec((1, D_MODEL), const),
        ],
        out_specs=pl.BlockSpec((tm, D_MODEL), lambda i: (i, 0)),
        out_shape=jax.ShapeDtypeStruct((n, D_MODEL), F32),
        compiler_params=_cparams("parallel"),
        name="s5_glu_ln",
    )(x, y, d_skip.reshape(1, -1), w_glu.astype(BF16), g.reshape(1, -1), b.reshape(1, -1))


def _ffn_kernel(x_ref, p1_ref, p2_ref, wu_ref, cw_ref, cb_ref, wd_ref, g_ref, b_ref,
                o_ref, st_ref, carry_ref, *, period, chained):
    i = pl.program_id(0)
    x = x_ref[...]
    xb = x.astype(BF16)
    tm = x.shape[0]
    row = lax.broadcasted_iota(jnp.int32, (tm, 1), 0)
    t = row % period
    if chained:
        @pl.when(i == 0)
        def _():
            carry_ref[...] = jnp.zeros_like(carry_ref)

    def conv(cols):
        h = jnp.dot(xb, wu_ref[:, cols], preferred_element_type=F32)
        h1 = jnp.where(t >= 1, pltpu.roll(h, 1, axis=0), 0.0)
        h2 = jnp.where(t >= 2, pltpu.roll(h, 2, axis=0), 0.0)
        if chained:
            c0 = carry_ref[6:7, cols]
            c1 = carry_ref[7:8, cols]
            h1 = h1 + jnp.where(row == 0, c1, 0.0)
            h2 = h2 + jnp.where(row == 0, c0, 0.0) + jnp.where(row == 1, c1, 0.0)
            carry_ref[:, cols] = h[tm - 8:, :]
            st_ref[:, cols] = h[tm - 8:, :]
        else:
            h1 = h1 + p1_ref[:, cols]
            h2 = h2 + p2_ref[:, cols]
            st_ref[:, cols] = h
        return cb_ref[:, cols] + cw_ref[0:1, cols] * h2 + cw_ref[1:2, cols] * h1 + cw_ref[2:3, cols] * h

    f = jnp.zeros((tm, D_MODEL), F32)
    for c in range(D_FF // FFN_CHUNK):
        ca = slice(c * FFN_CHUNK, (c + 1) * FFN_CHUNK)
        cv = slice(D_FF + c * FFN_CHUNK, D_FF + (c + 1) * FFN_CHUNK)
        act = jax.nn.silu(conv(ca)) * conv(cv)
        f = f + jnp.dot(act.astype(BF16), wd_ref[ca, :], preferred_element_type=F32)
    o_ref[...] = _layer_norm(ALPHA * x + f, g_ref[...], b_ref[...])


def _ffn(x, buf, w_up, conv_w, conv_b, w_down, g, b, *, seq_len):
    n = x.shape[0]
    n_seq = n // seq_len
    f2 = 2 * D_FF
    chained = buf is None
    if chained:
        assert n_seq == 1
        tm = _row_tile(n, 512)
        p1 = p2 = jnp.zeros((8, f2), F32)
        pspec = pl.BlockSpec((8, f2), lambda i: (0, 0))
        st_shape, st_spec = (8, f2), pl.BlockSpec((8, f2), lambda i: (0, 0))
    else:
        tm = n
        z = jnp.zeros((n_seq, seq_len, f2), F32)
        p1 = z.at[:, 0].set(buf[:, 1]).reshape(n, f2)
        p2 = z.at[:, 0].set(buf[:, 0]).at[:, 1].set(buf[:, 1]).reshape(n, f2)
        pspec = pl.BlockSpec((tm, f2), lambda i: (i, 0))
        st_shape, st_spec = (n, f2), pl.BlockSpec((tm, f2), lambda i: (i, 0))
    const = lambda i: (0, 0)
    out, st = pl.pallas_call(
        functools.partial(_ffn_kernel, period=seq_len, chained=chained),
        grid=(n // tm,),
        in_specs=[
            pl.BlockSpec((tm, D_MODEL), lambda i: (i, 0)),
            pspec, pspec,
            pl.BlockSpec((D_MODEL, f2), const, pipeline_mode=pl.Buffered(1)),
            pl.BlockSpec((CONV_W, f2), const),
            pl.BlockSpec((1, f2), const),
            pl.BlockSpec((D_FF, D_MODEL), const, pipeline_mode=pl.Buffered(1)),
            pl.BlockSpec((1, D_MODEL), const),
            pl.BlockSpec((1, D_MODEL), const),
        ],
        out_specs=[pl.BlockSpec((tm, D_MODEL), lambda i: (i, 0)), st_spec],
        out_shape=[jax.ShapeDtypeStruct((n, D_MODEL), F32), jax.ShapeDtypeStruct(st_shape, F32)],
        scratch_shapes=[pltpu.VMEM((8, f2), F32)],
        compiler_params=_cparams("arbitrary"),
        name="conv_ffn",
    )(x, p1, p2, w_up.astype(BF16), conv_w, conv_b.reshape(1, -1), w_down.astype(BF16),
      g.reshape(1, -1), b.reshape(1, -1))
    if chained:
        state = st[6:8][None]
    else:
        state = st.reshape(n_seq, seq_len, f2)[:, seq_len - 2:]
    return out, state


TQ = 128
TK = 512


def _perm_heads():
    idx = jnp.arange(D_MODEL).reshape(N_KV, HEADS_PER_KV, HEAD_DIM)
    return idx.transpose(1, 0, 2).reshape(-1)


def _nsa_proj_kernel(x_ref, wq_ref, wkv_ref, wg_ref, *refs, rows_t):
    xb = x_ref[...].astype(BF16)
    q = jnp.dot(xb, wq_ref[...], preferred_element_type=F32) * (HEAD_DIM ** -0.5)
    kv = jnp.dot(xb, wkv_ref[...], preferred_element_type=F32)
    gates = jax.nn.sigmoid(jnp.dot(xb, wg_ref[...], preferred_element_type=F32))
    if rows_t:
        wkvt_ref, q_ref, kc_ref, vc_ref, kvb_ref, g_ref = refs[:6]
        kvt = lax.dot_general(wkvt_ref[...], xb, (((1,), (1,)), ((), ())), preferred_element_type=F32)
        for j, r in enumerate(refs[6:]):
            r[...] = kvt[j * KV_DIM:(j + 1) * KV_DIM, :]
    else:
        q_ref, kc_ref, vc_ref, kvb_ref, g_ref = refs[:5]
        for j, r in enumerate(refs[5:]):
            r[...] = kv[:, (j + 2) * KV_DIM:(j + 3) * KV_DIM]
    q_ref[...] = q.astype(BF16)
    kc_ref[...] = kv[:, :KV_DIM]
    vc_ref[...] = kv[:, KV_DIM:2 * KV_DIM]
    kvb_ref[...] = kv[:, 2 * KV_DIM:].astype(BF16)
    g_ref[...] = gates.astype(BF16)


def _gate_columns():
    br = jnp.arange(3)[:, None, None, None]
    h = jnp.arange(HEADS_PER_KV)[None, :, None, None]
    g = jnp.arange(N_KV)[None, None, :, None]
    idx = (g * HEADS_PER_KV + h) * 3 + br + jnp.zeros((1, 1, 1, HEAD_DIM), jnp.int32)
    return idx.reshape(-1)


def _nsa_proj(x, w_in, *, rows_t):
    n = x.shape[0]
    tm = _row_tile(n, 512)
    hd = N_HEADS * HEAD_DIM
    wq = w_in[:, :hd][:, _perm_heads()].astype(BF16)
    wkv = w_in[:, hd:hd + 6 * KV_DIM].astype(BF16)
    wg = w_in[:, hd + 6 * KV_DIM:][:, _gate_columns()].astype(BF16)
    const = lambda i: (0, 0)
    rows = lambda w: pl.BlockSpec((tm, w), lambda i: (i, 0))
    in_specs = [rows(D_MODEL), pl.BlockSpec((D_MODEL, hd), const), pl.BlockSpec((D_MODEL, 6 * KV_DIM), const),
                pl.BlockSpec((D_MODEL, 3 * hd), const)]
    out_specs = [rows(hd), rows(KV_DIM), rows(KV_DIM), rows(4 * KV_DIM), rows(3 * hd)]
    out_shape = [jax.ShapeDtypeStruct((n, hd), BF16), jax.ShapeDtypeStruct((n, KV_DIM), F32),
                 jax.ShapeDtypeStruct((n, KV_DIM), F32), jax.ShapeDtypeStruct((n, 4 * KV_DIM), BF16),
                 jax.ShapeDtypeStruct((n, 3 * hd), BF16)]
    args = [x, wq, wkv, wg]
    if rows_t:
        in_specs.append(pl.BlockSpec((6 * KV_DIM, D_MODEL), const))
        args.append(wkv.T)
        out_specs += [pl.BlockSpec((KV_DIM, tm), lambda i: (0, i))] * 6
        out_shape += [jax.ShapeDtypeStruct((KV_DIM, n), F32)] * 6
    else:
        out_specs += [rows(KV_DIM)] * 4
        out_shape += [jax.ShapeDtypeStruct((n, KV_DIM), F32)] * 4
    return pl.pallas_call(
        functools.partial(_nsa_proj_kernel, rows_t=rows_t),
        grid=(n // tm,),
        in_specs=in_specs,
        out_specs=out_specs,
        out_shape=out_shape,
        compiler_params=_cparams("parallel"),
        name="nsa_proj",
    )(*args)


def _cmp_weights(pe, w1, b1, w2):
    eye = jnp.eye(N_KV, dtype=F32)
    w1b = jnp.einsum('jsdh,ge->jsgdeh', w1.reshape(2, CMP_STRIDE, HEAD_DIM, CMP_HIDDEN), eye)
    w1b = w1b.reshape(2, CMP_STRIDE * KV_DIM, N_KV * CMP_HIDDEN).astype(BF16)
    w2b = jnp.einsum('hd,ge->ghed', w2, eye).reshape(N_KV * CMP_HIDDEN, KV_DIM).astype(BF16)
    peb = jnp.broadcast_to(pe.reshape(2, CMP_STRIDE, 1, HEAD_DIM), (2, CMP_STRIDE, N_KV, HEAD_DIM))
    peb = peb.reshape(2, 1, CMP_STRIDE * KV_DIM)
    b1b = jnp.tile(b1, N_KV).reshape(1, N_KV * CMP_HIDDEN)
    return peb, w1b, b1b, w2b


def _compress_rows(x, carry_ref, pe_ref, w1_ref, b1_ref, w2_ref):
    p0 = jnp.dot((x + pe_ref[0]).astype(BF16), w1_ref[0], preferred_element_type=F32)
    p1 = jnp.dot((x + pe_ref[1]).astype(BF16), w1_ref[1], preferred_element_type=F32)
    row = lax.broadcasted_iota(jnp.int32, (x.shape[0], 1), 0)
    p0s = jnp.where(row == 0, carry_ref[7:8, :], pltpu.roll(p0, 1, axis=0))
    carry_ref[...] = p0[x.shape[0] - 8:, :]
    h = b1_ref[...] + p0s + p1
    return jnp.dot(jax.nn.gelu(h).astype(BF16), w2_ref[...], preferred_element_type=F32)


def _compress_prompt_kernel(x_ref, pe_ref, w1_ref, b1_ref, w2_ref, o_ref, carry_ref):
    @pl.when(pl.program_id(0) == 0)
    def _():
        carry_ref[...] = jnp.zeros_like(carry_ref)
    o_ref[...] = _compress_rows(x_ref[...], carry_ref, pe_ref, w1_ref, b1_ref, w2_ref).astype(o_ref.dtype)


def _compress_prompt(rows, cw):
    peb, w1b, b1b, w2b = cw
    n = rows.shape[0]
    nc = n // CMP_STRIDE
    wide = CMP_STRIDE * KV_DIM
    x = rows.reshape(nc, wide)
    tc = _row_tile(nc, 256)
    return pl.pallas_call(
        _compress_prompt_kernel,
        grid=(nc // tc,),
        in_specs=[pl.BlockSpec((tc, wide), lambda i: (i, 0)),
                  pl.BlockSpec((2, 1, wide), lambda i: (0, 0, 0)),
                  pl.BlockSpec((2, wide, N_KV * CMP_HIDDEN), lambda i: (0, 0, 0)),
                  pl.BlockSpec((1, N_KV * CMP_HIDDEN), lambda i: (0, 0)),
                  pl.BlockSpec((N_KV * CMP_HIDDEN, KV_DIM), lambda i: (0, 0))],
        out_specs=pl.BlockSpec((tc, KV_DIM), lambda i: (i, 0)),
        out_shape=jax.ShapeDtypeStruct((nc, KV_DIM), BF16),
        scratch_shapes=[pltpu.VMEM((8, N_KV * CMP_HIDDEN), F32)],
        compiler_params=_cparams("arbitrary"),
        name="nsa_compress",
    )(x, peb, w1b, b1b, w2b)


def _masked_softmax_rows(s, valid):
    s = jnp.where(valid, s, NEG_BIG)
    m = jnp.max(s, -1, keepdims=True)
    e = jnp.where(valid, jnp.exp(s - m), 0.0)
    den = jnp.sum(e, -1, keepdims=True)
    return e / jnp.where(den > 0, den, 1.0)


def _select_blocks(imp, cur, n_sel, axis=1):
    ns = imp.shape[axis]
    jj = lax.broadcasted_iota(jnp.int32, (1, ns) if axis == 1 else (ns, 1), axis)
    forced = (jj == 0) | ((jj <= cur) & (jj >= cur - 1))
    v = jnp.where(jj > cur, -FORCE, jnp.where(forced, FORCE, imp))
    sel = jnp.zeros(imp.shape, jnp.bool_)
    for _ in range(min(n_sel, ns)):
        m = jnp.max(v, axis, keepdims=True)
        idx = jnp.min(jnp.where(v == m, jj, ns), axis, keepdims=True)
        hit = jj == idx
        sel = sel | hit
        v = jnp.where(hit, -jnp.inf, v)
    return sel


def _nsa_attn_prompt_kernel(q_ref, gate_ref, ks_ref, vs_ref, *rest):
    kw_refs, vw_refs = rest[0:5], rest[5:10]
    kc_ref, vc_ref, c2st_ref, o_ref, qg_ref, selb_ref, m_ref, l_ref, acc_ref, oo_ref = rest[10:]
    i = pl.program_id(0)
    nh = HEADS_PER_KV
    nc = kc_ref.shape[0]
    ns = c2st_ref.shape[0]
    rows = nh * TQ
    hd = N_HEADS * HEAD_DIM
    q = q_ref[...]
    qh = jnp.concatenate([q[:, KV_DIM * h:KV_DIM * (h + 1)] for h in range(nh)], axis=0)
    lane_g = lax.broadcasted_iota(jnp.int32, (1, KV_DIM), 1) // HEAD_DIM
    qpos_r = i * TQ + lax.broadcasted_iota(jnp.int32, (rows, 1), 0) % TQ
    qpos_q = i * TQ + lax.broadcasted_iota(jnp.int32, (TQ, 1), 0)
    cur_t = (i * TQ + lax.broadcasted_iota(jnp.int32, (1, TQ), 1)) // SEL_BLOCK
    nt = (((i + 1) * TQ + TK - 1) // TK)
    nt_dims = (((1,), (1,)), ((), ()))

    def gate_rows(br):
        return jnp.concatenate([gate_ref[:, br * hd + KV_DIM * h:br * hd + KV_DIM * (h + 1)] for h in range(nh)],
                               axis=0).astype(F32)

    kwin = jnp.concatenate([r[...] for r in kw_refs], axis=0)
    vwin = jnp.concatenate([r[...] for r in vw_refs], axis=0)
    kw_pos = i * TQ - WINDOW + lax.broadcasted_iota(jnp.int32, (1, WINDOW + TQ), 1)
    dlt = qpos_r - kw_pos
    win_ok = (dlt >= 0) & (dlt < WINDOW) & (kw_pos >= 0)
    n_idx = lax.broadcasted_iota(jnp.int32, (1, nc), 1)
    cmp_ok = (n_idx >= 1) & (n_idx * CMP_STRIDE + (CMP_STRIDE - 1) <= qpos_r)
    g_c, g_w = gate_rows(0), gate_rows(2)

    for g in range(N_KV):
        qg = jnp.where(lane_g == g, qh, jnp.zeros_like(qh))
        qg_ref[g] = qg
        s_c = lax.dot_general(qg, kc_ref[...], nt_dims, preferred_element_type=F32)
        p_c = _masked_softmax_rows(s_c, cmp_ok)
        o_c = jnp.dot(p_c.astype(BF16), vc_ref[...], preferred_element_type=F32)
        pcs = p_c[0:TQ]
        for h in range(1, nh):
            pcs = pcs + p_c[h * TQ:(h + 1) * TQ]
        impt = jnp.zeros((ns, TQ), F32)
        rem = pcs
        for _ in range(3):
            part = rem.astype(BF16)
            rem = rem - part.astype(F32)
            impt = impt + lax.dot_general(c2st_ref[...], part, nt_dims, preferred_element_type=F32)
        sel_t = _select_blocks(impt, cur_t, N_SEL, axis=0)
        selb_ref[g] = jnp.transpose(sel_t.astype(F32)).astype(BF16)
        s_w = lax.dot_general(qg, kwin, nt_dims, preferred_element_type=F32)
        p_w = _masked_softmax_rows(s_w, win_ok)
        o_w = jnp.dot(p_w.astype(BF16), vwin, preferred_element_type=F32)
        oo_ref[g] = g_c * o_c + g_w * o_w
        m_ref[g] = jnp.full((rows, 128), NEG_BIG, F32)
        l_ref[g] = jnp.zeros((rows, 128), F32)
        acc_ref[g] = jnp.zeros((rows, KV_DIM), F32)

    j_row = lax.broadcasted_iota(jnp.int32, (ns, TK), 0)
    j_of_lane = lax.broadcasted_iota(jnp.int32, (ns, TK), 1) // SEL_BLOCK
    key_lane = lax.broadcasted_iota(jnp.int32, (1, TK), 1)

    def body(kt, carry):
        k0 = pl.multiple_of(kt * TK, TK)
        kblk = ks_ref[pl.ds(k0, TK), :]
        vblk = vs_ref[pl.ds(k0, TK), :]
        onehot = (j_row == kt * (TK // SEL_BLOCK) + j_of_lane).astype(BF16)
        causal = k0 + key_lane <= qpos_q
        for g in range(N_KV):
            picked = jnp.dot(selb_ref[g], onehot, preferred_element_type=F32)
            bias = jnp.where((picked > 0.5) & causal, 0.0, NEG_BIG)
            s = lax.dot_general(qg_ref[g], kblk, nt_dims, preferred_element_type=F32)
            s = s + jnp.concatenate([bias] * nh, axis=0)
            m_prev = m_ref[g]
            m_new = jnp.maximum(m_prev, jnp.max(s, -1, keepdims=True))
            alpha = jnp.exp(m_prev - m_new)
            p = jnp.exp(s - jnp.concatenate([m_new] * (TK // 128), axis=1))
            l_ref[g] = alpha * l_ref[g] + jnp.sum(p, -1, keepdims=True)
            acc_ref[g] = (jnp.concatenate([alpha] * (KV_DIM // 128), axis=1) * acc_ref[g]
                          + jnp.dot(p.astype(BF16), vblk, preferred_element_type=F32))
            m_ref[g] = m_new
        return carry

    lax.fori_loop(0, nt, body, 0)

    g_s = gate_rows(1)
    total = jnp.zeros((rows, KV_DIM), F32)
    for g in range(N_KV):
        o_s = acc_ref[g] / jnp.concatenate([l_ref[g]] * (KV_DIM // 128), axis=1)
        total = jnp.where(lane_g == g, oo_ref[g] + g_s * o_s, total)
    for h in range(nh):
        o_ref[:, KV_DIM * h:KV_DIM * (h + 1)] = total[h * TQ:(h + 1) * TQ].astype(o_ref.dtype)


def _cmp_to_sel(nc, ns):
    blk = jnp.arange(nc)[:, None] - 1
    i = blk * CMP_STRIDE
    j = jnp.arange(ns)[None, :] * SEL_BLOCK
    return ((blk >= 0) & (i < j + SEL_BLOCK) & (i + CMP_LEN > j)).astype(F32)


def _nsa_attn_prompt(q, gates, kvb, kc, vc):
    n = q.shape[0]
    nc, ns = n // CMP_STRIDE, n // SEL_BLOCK
    assert n % TK == 0 and WINDOW % TQ == 0
    wpad = jnp.pad(kvb[:, 2 * KV_DIM:], ((WINDOW, 0), (0, 0)))
    c2st = _cmp_to_sel(nc, ns).T.astype(BF16)
    nwb = WINDOW // TQ + 1
    rows = HEADS_PER_KV * TQ
    const2 = lambda i: (0, 0)
    one = pl.Buffered(1)
    in_specs = [
        pl.BlockSpec((TQ, D_MODEL), lambda i: (i, 0)),
        pl.BlockSpec((TQ, 3 * D_MODEL), lambda i: (i, 0)),
        pl.BlockSpec((n, KV_DIM), lambda i: (0, 0), pipeline_mode=one),
        pl.BlockSpec((n, KV_DIM), lambda i: (0, 1), pipeline_mode=one),
    ]
    in_specs += [pl.BlockSpec((TQ, KV_DIM), functools.partial(lambda i, k: (i + k, 0), k=k)) for k in range(nwb)]
    in_specs += [pl.BlockSpec((TQ, KV_DIM), functools.partial(lambda i, k: (i + k, 1), k=k)) for k in range(nwb)]
    in_specs += [
        pl.BlockSpec((nc, KV_DIM), const2, pipeline_mode=one),
        pl.BlockSpec((nc, KV_DIM), const2, pipeline_mode=one),
        pl.BlockSpec((ns, nc), const2, pipeline_mode=one),
    ]
    return pl.pallas_call(
        _nsa_attn_prompt_kernel,
        grid=(n // TQ,),
        in_specs=in_specs,
        out_specs=pl.BlockSpec((TQ, D_MODEL), lambda i: (i, 0)),
        out_shape=jax.ShapeDtypeStruct((n, D_MODEL), BF16),
        scratch_shapes=[pltpu.VMEM((N_KV, rows, KV_DIM), BF16), pltpu.VMEM((N_KV, TQ, ns), BF16),
                        pltpu.VMEM((N_KV, rows, 128), F32), pltpu.VMEM((N_KV, rows, 128), F32),
                        pltpu.VMEM((N_KV, rows, KV_DIM), F32), pltpu.VMEM((N_KV, rows, KV_DIM), F32)],
        compiler_params=_cparams("parallel"),
        name="nsa_attn_prompt",
    )(q, gates, kvb, kvb, *([wpad] * (2 * nwb)), kc, vc, c2st)


def _out_proj_ln_kernel(x_ref, o_ref, w_ref, g_ref, b_ref, y_ref):
    y = jnp.dot(o_ref[...], w_ref[...], preferred_element_type=F32)
    y_ref[...] = _layer_norm(ALPHA * x_ref[...] + y, g_ref[...], b_ref[...])


def _out_proj_ln(x, o, w_o, g, b):
    n = x.shape[0]
    tm = _row_tile(n, 512)
    const = lambda i: (0, 0)
    return pl.pallas_call(
        _out_proj_ln_kernel,
        grid=(n // tm,),
        in_specs=[pl.BlockSpec((tm, D_MODEL), lambda i: (i, 0)), pl.BlockSpec((tm, D_MODEL), lambda i: (i, 0)),
                  pl.BlockSpec((D_MODEL, D_MODEL), const), pl.BlockSpec((1, D_MODEL), const),
                  pl.BlockSpec((1, D_MODEL), const)],
        out_specs=pl.BlockSpec((tm, D_MODEL), lambda i: (i, 0)),
        out_shape=jax.ShapeDtypeStruct((n, D_MODEL), F32),
        compiler_params=_cparams("parallel"),
        name="nsa_out_ln",
    )(x, o, w_o[_perm_heads(), :].astype(BF16), g.reshape(1, -1), b.reshape(1, -1))


def _nsa_prompt_layer(x, w_in, pe, w1, b1, w2, w_o, g, b):
    q, k_c, v_c, kvb, gates, *rows_t = _nsa_proj(x, w_in, rows_t=True)
    kc = _compress_prompt(k_c, _cmp_weights(pe[0], w1[0], b1[0], w2[0]))
    vc = _compress_prompt(v_c, _cmp_weights(pe[1], w1[1], b1[1], w2[1]))
    o = _nsa_attn_prompt(q, gates, kvb, kc, vc)
    y = _out_proj_ln(x, o, w_o, g, b)
    return y, tuple(rows_t)


def _compress_sample_kernel(pt_ref, *refs, n_pg):
    pages = refs[:n_pg]
    new_ref, pe_ref, w1_ref, b1_ref, w2_ref, o_ref, carry_ref = refs[n_pg:]
    s = pl.program_id(1)
    last = pl.num_programs(1) - 1

    @pl.when(s == 0)
    def _():
        carry_ref[...] = jnp.zeros_like(carry_ref)
    first = jnp.where(s == last, new_ref[0], pages[0][0])
    x = jnp.concatenate([first] + [p[0] for p in pages[1:]], axis=0)
    o_ref[0] = _compress_rows(x, carry_ref, pe_ref, w1_ref, b1_ref, w2_ref).astype(o_ref.dtype)


def _compress_sample(pool, page_table, new_rows, cw):
    peb, w1b, b1b, w2b = cw
    B, n_pages = page_table.shape
    t = new_rows.shape[1]
    assert t <= CMP_STRIDE
    wide = CMP_STRIDE * KV_DIM
    cpp = PAGE_SIZE // CMP_STRIDE
    n_pg = min(16, n_pages)
    assert n_pages % n_pg == 0
    steps = n_pages // n_pg + 1
    pool2 = pool.reshape(pool.shape[0], cpp, wide)
    newc = jnp.pad(new_rows.reshape(B, 1, t * KV_DIM), ((0, 0), (0, cpp - 1), (0, wide - t * KV_DIM)))

    def page_map(b, s, pt, k):
        return (pt[b * n_pages + jnp.minimum(s * n_pg + k, n_pages - 1)], 0, 0)

    tc = n_pg * cpp
    grid_spec = pltpu.PrefetchScalarGridSpec(
        num_scalar_prefetch=1,
        grid=(B, steps),
        in_specs=[pl.BlockSpec((1, cpp, wide), functools.partial(page_map, k=k)) for k in range(n_pg)] + [
            pl.BlockSpec((1, cpp, wide), lambda b, s, pt: (b, 0, 0)),
            pl.BlockSpec((2, 1, wide), lambda b, s, pt: (0, 0, 0)),
            pl.BlockSpec((2, wide, N_KV * CMP_HIDDEN), lambda b, s, pt: (0, 0, 0)),
            pl.BlockSpec((1, N_KV * CMP_HIDDEN), lambda b, s, pt: (0, 0)),
            pl.BlockSpec((N_KV * CMP_HIDDEN, KV_DIM), lambda b, s, pt: (0, 0))],
        out_specs=pl.BlockSpec((1, tc, KV_DIM), lambda b, s, pt: (b, s, 0)),
        scratch_shapes=[pltpu.VMEM((8, N_KV * CMP_HIDDEN), F32)],
    )
    return pl.pallas_call(
        functools.partial(_compress_sample_kernel, n_pg=n_pg),
        grid_spec=grid_spec,
        out_shape=jax.ShapeDtypeStruct((B, steps * tc, KV_DIM), BF16),
        compiler_params=_cparams("parallel", "arbitrary"),
        name="nsa_compress_paged",
    )(page_table.reshape(-1), *([pool2] * n_pg), newc, peb, w1b, b1b, w2b)


def _nsa_attn_sample_kernel(pt_ref, q_ref, gate_ref, new_ref, *refs, n_pg, past, t_len, n_blk):
    kpages, vpages = refs[:n_pg], refs[n_pg:2 * n_pg]
    kwb_ref, vwb_ref, kc_ref, vc_ref, c2s_ref, o_ref, qs_ref, selb_ref, m_ref, l_ref, acc_ref, oo_ref = refs[2 * n_pg:]
    s = pl.program_id(1)
    nh = HEADS_PER_KV
    rows = nh * N_KV * t_len
    nc = kc_ref.shape[1]
    nsp = c2s_ref.shape[1]
    hp = lax.Precision.HIGHEST
    nt_dims = (((1,), (1,)), ((), ()))
    r_iota = lax.broadcasted_iota(jnp.int32, (rows, 1), 0)
    qpos_r = past + r_iota % t_len
    lane_g = lax.broadcasted_iota(jnp.int32, (1, KV_DIM), 1) // HEAD_DIM
    row_g = (r_iota // t_len) % N_KV

    def tile_rows(x, k):
        return jnp.concatenate([x] * k, axis=0)

    @pl.when(s == 0)
    def _():
        qf = q_ref[...].astype(F32)
        q128 = jnp.concatenate([tile_rows(qf[:, KV_DIM * h:KV_DIM * (h + 1)], N_KV) for h in range(nh)], axis=0)
        q128 = jnp.where(lane_g == row_g, q128, 0.0).astype(BF16)
        qs_ref[...] = q128
        hd = N_HEADS * HEAD_DIM
        gx = [jnp.concatenate([tile_rows(gate_ref[:, br * hd + KV_DIM * h:br * hd + KV_DIM * (h + 1)].astype(F32), N_KV)
                               for h in range(nh)], axis=0) for br in range(3)]
        new = new_ref[...]
        n_idx = lax.broadcasted_iota(jnp.int32, (1, nc), 1)
        cmp_ok = (n_idx >= 1) & (n_idx <= n_blk) & (n_idx * CMP_STRIDE + (CMP_STRIDE - 1) <= qpos_r)
        s_c = lax.dot_general(q128, kc_ref[0], nt_dims, preferred_element_type=F32)
        p_c = _masked_softmax_rows(s_c, cmp_ok)
        o_c = jnp.dot(p_c.astype(BF16), vc_ref[0], preferred_element_type=F32)
        gq = N_KV * t_len
        pcs = p_c[0:gq]
        for h in range(1, nh):
            pcs = pcs + p_c[h * gq:(h + 1) * gq]
        imp = jnp.dot(pcs, c2s_ref[...], precision=hp, preferred_element_type=F32)
        cur = (past + lax.broadcasted_iota(jnp.int32, (gq, 1), 0) % t_len) // SEL_BLOCK
        sel = _select_blocks(imp, cur, N_SEL).astype(F32)
        selb_ref[...] = tile_rows(sel, nh).astype(BF16)
        kwin = jnp.concatenate([kwb_ref[0], new[:, 2 * KV_DIM:3 * KV_DIM]], axis=0).astype(BF16)
        vwin = jnp.concatenate([vwb_ref[0], new[:, 3 * KV_DIM:]], axis=0).astype(BF16)
        wb = kwb_ref.shape[1]
        kw_pos = past - wb + lax.broadcasted_iota(jnp.int32, (1, wb + t_len), 1)
        dlt = qpos_r - kw_pos
        win_ok = (dlt >= 0) & (dlt < WINDOW) & (kw_pos >= 0)
        s_w = lax.dot_general(q128, kwin, nt_dims, preferred_element_type=F32)
        p_w = _masked_softmax_rows(s_w, win_ok)
        o_w = jnp.dot(p_w.astype(BF16), vwin, preferred_element_type=F32)
        oo_ref[0] = gx[0] * o_c + gx[2] * o_w
        oo_ref[1] = gx[1]
        knew = new[:, 0:KV_DIM].astype(BF16)
        vnew = new[:, KV_DIM:2 * KV_DIM].astype(BF16)
        s_n = lax.dot_general(q128, knew, nt_dims, preferred_element_type=F32)
        jn = lax.broadcasted_iota(jnp.int32, (nsp, t_len), 0)
        picked = jnp.dot(selb_ref[...], (jn == past // SEL_BLOCK).astype(BF16), preferred_element_type=F32)
        ok = (picked > 0.5) & (past + lax.broadcasted_iota(jnp.int32, (1, t_len), 1) <= qpos_r)
        s_n = jnp.where(ok, s_n, NEG_BIG)
        m0 = jnp.max(s_n, -1, keepdims=True)
        p0 = jnp.where(ok, jnp.exp(s_n - m0), 0.0)
        m_ref[...] = m0
        l_ref[...] = jnp.sum(p0, -1, keepdims=True)
        acc_ref[...] = jnp.dot(p0.astype(BF16), vnew, preferred_element_type=F32)

    tk = n_pg * PAGE_SIZE
    qs = qs_ref[...]
    sc = jnp.concatenate([jnp.dot(qs, p[0].astype(BF16), preferred_element_type=F32) for p in kpages], axis=1)
    j_row = lax.broadcasted_iota(jnp.int32, (nsp, tk), 0)
    j_of_lane = lax.broadcasted_iota(jnp.int32, (nsp, tk), 1) // SEL_BLOCK
    onehot = (j_row == s * (tk // SEL_BLOCK) + j_of_lane).astype(BF16)
    picked = jnp.dot(selb_ref[...], onehot, preferred_element_type=F32)
    sc = jnp.where(picked > 0.5, sc, NEG_BIG)
    m = m_ref[...]
    m_new = jnp.maximum(m, jnp.max(sc, -1, keepdims=True))
    alpha = jnp.exp(m - m_new)
    p = jnp.where(picked > 0.5, jnp.exp(sc - m_new), 0.0)
    l_ref[...] = alpha * l_ref[...] + jnp.sum(p, -1, keepdims=True)
    pb = p.astype(BF16)
    pv = jnp.zeros((rows, KV_DIM), F32)
    for k, vp in enumerate(vpages):
        pv = pv + lax.dot_general(pb[:, k * PAGE_SIZE:(k + 1) * PAGE_SIZE], vp[0].astype(BF16), nt_dims,
                                  preferred_element_type=F32)
    acc_ref[...] = alpha * acc_ref[...] + pv
    m_ref[...] = m_new

    @pl.when(s == pl.num_programs(1) - 1)
    def _():
        l = l_ref[...]
        o_s = acc_ref[...] / jnp.where(l > 0, l, 1.0)
        total = oo_ref[0] + oo_ref[1] * o_s
        gq = N_KV * t_len
        for h in range(nh):
            oh = jnp.zeros((t_len, KV_DIM), F32)
            for g in range(N_KV):
                oh = jnp.where(lane_g == g, total[h * gq + g * t_len:h * gq + (g + 1) * t_len], oh)
            o_ref[:, KV_DIM * h:KV_DIM * (h + 1)] = oh.astype(o_ref.dtype)


def _nsa_attn_sample(q, gates, new, page_table, pool_ks, pool_vs, buf_kw, buf_vw, kc, vc, *, t_len):
    B, n_pages = page_table.shape
    past = n_pages * PAGE_SIZE
    total = past + t_len
    n_blk = -(-total // CMP_STRIDE) - 1
    ns = -(-total // SEL_BLOCK)
    nsp = -(-ns // 128) * 128
    nc = kc.shape[1]
    c2s = jnp.pad(_cmp_to_sel(nc, ns), ((0, 0), (0, nsp - ns)))
    n_pg = min(8, n_pages)
    assert n_pages % n_pg == 0 and t_len % 8 == 0
    steps = n_pages // n_pg
    rows = N_HEADS * t_len
    wb = buf_kw.shape[1]
    one = pl.Buffered(1)

    def page_map(b, s, pt, k):
        return (pt[b * n_pages + s * n_pg + k], 0, 0)

    per_b = lambda w: pl.BlockSpec((t_len, w), lambda b, s, pt: (b, 0))
    grid_spec = pltpu.PrefetchScalarGridSpec(
        num_scalar_prefetch=1,
        grid=(B, steps),
        in_specs=[per_b(D_MODEL), per_b(3 * D_MODEL), per_b(4 * KV_DIM)]
        + [pl.BlockSpec((1, KV_DIM, PAGE_SIZE), functools.partial(page_map, k=k)) for k in range(n_pg)] * 2
        + [pl.BlockSpec((1, wb, KV_DIM), lambda b, s, pt: (b, 0, 0)),
           pl.BlockSpec((1, wb, KV_DIM), lambda b, s, pt: (b, 0, 0)),
           pl.BlockSpec((1, nc, KV_DIM), lambda b, s, pt: (b, 0, 0)),
           pl.BlockSpec((1, nc, KV_DIM), lambda b, s, pt: (b, 0, 0)),
           pl.BlockSpec((nc, nsp), lambda b, s, pt: (0, 0), pipeline_mode=one)],
        out_specs=pl.BlockSpec((t_len, D_MODEL), lambda b, s, pt: (b, 0)),
        scratch_shapes=[pltpu.VMEM((rows, KV_DIM), BF16), pltpu.VMEM((rows, nsp), BF16),
                        pltpu.VMEM((rows, 1), F32), pltpu.VMEM((rows, 1), F32), pltpu.VMEM((rows, KV_DIM), F32),
                        pltpu.VMEM((2, rows, KV_DIM), F32)],
    )
    return pl.pallas_call(
        functools.partial(_nsa_attn_sample_kernel, n_pg=n_pg, past=past, t_len=t_len, n_blk=n_blk),
        grid_spec=grid_spec,
        out_shape=jax.ShapeDtypeStruct((B * t_len, D_MODEL), BF16),
        compiler_params=_cparams("parallel", "arbitrary"),
        name="nsa_attn_sample",
    )(page_table.reshape(-1), q, gates, new, *([pool_ks] * n_pg), *([pool_vs] * n_pg), buf_kw, buf_vw, kc, vc, c2s)


def _nsa_sample_layer(x, page_table, pool_kc, pool_vc, pool_ks, pool_vs, buf_kw, buf_vw,
                      w_in, pe, w1, b1, w2, w_o, g, b, *, t_len):
    B = page_table.shape[0]
    q, k_c, v_c, _, gates, k_s, v_s, k_w, v_w = _nsa_proj(x, w_in, rows_t=False)
    flat = lambda p: p.reshape(p.shape[0], p.shape[1], KV_DIM)
    pages_t = lambda p: p.transpose(0, 2, 3, 1).reshape(p.shape[0], KV_DIM, p.shape[1])
    kc = _compress_sample(flat(pool_kc), page_table, k_c.reshape(B, t_len, KV_DIM), _cmp_weights(pe[0], w1[0], b1[0], w2[0]))
    vc = _compress_sample(flat(pool_vc), page_table, v_c.reshape(B, t_len, KV_DIM), _cmp_weights(pe[1], w1[1], b1[1], w2[1]))
    new = jnp.concatenate([k_s, v_s, k_w, v_w], axis=1)
    o = _nsa_attn_sample(q, gates, new, page_table, pages_t(pool_ks), pages_t(pool_vs), flat(buf_kw), flat(buf_vw),
                         kc, vc, t_len=t_len)
    y = _out_proj_ln(x, o, w_o, g, b)
    return y, (k_c, v_c, k_s, v_s, k_w, v_w), o


def kernel(x_prompt, x_sample, state_s5_re, state_s5_im, cache_k_cmp, cache_v_cmp, cache_k_sel, cache_v_sel,
           cache_k_win, cache_v_win, state_ffn_conv, page_table,
           s5_a_re, s5_a_im, s5_log_dt, s5_b_re, s5_b_im, s5_c_re, s5_c_im, s5_d, s5_w_glu,
           nsa_w_in, nsa_cmp_pe, nsa_cmp_w1, nsa_cmp_b1, nsa_cmp_w2, nsa_w_o,
           ffn_w_up, ffn_conv_w, ffn_conv_b, ffn_w_down,
           ln_mix_g, ln_mix_b, ln_ffn_g, ln_ffn_b):
    assert s5_a_re.shape[0] == 1 and nsa_w_in.shape[0] == 1 and ffn_w_up.shape[0] == DEPTH
    bsz, L, _ = x_prompt.shape
    B, t_len, _ = x_sample.shape
    xp, xs, st_p, st_s = _s5_layer(x_prompt, x_sample, state_s5_re[0], state_s5_im[0], s5_a_re[0], s5_a_im[0],
                                   s5_log_dt[0], s5_b_re[0], s5_b_im[0], s5_c_re[0], s5_c_im[0], s5_d[0],
                                   s5_w_glu[0], ln_mix_g[0], ln_mix_b[0])
    fw = (ffn_w_up[0], ffn_conv_w[0], ffn_conv_b[0], ffn_w_down[0], ln_ffn_g[0], ln_ffn_b[0])
    xp, cp0 = _ffn(xp, None, *fw, seq_len=L)
    xs, cs0 = _ffn(xs, state_ffn_conv[0], *fw, seq_len=t_len)
    nw = (nsa_w_in[0], nsa_cmp_pe[0], nsa_cmp_w1[0], nsa_cmp_b1[0], nsa_cmp_w2[0], nsa_w_o[0],
          ln_mix_g[1], ln_mix_b[1])
    xp, rows_p = _nsa_prompt_layer(xp, *nw)
    xs, rows_s, _ = _nsa_sample_layer(xs, page_table, cache_k_cmp[0], cache_v_cmp[0], cache_k_sel[0], cache_v_sel[0],
                                      cache_k_win[0], cache_v_win[0], *nw, t_len=t_len)
    fw = (ffn_w_up[1], ffn_conv_w[1], ffn_conv_b[1], ffn_w_down[1], ln_ffn_g[1], ln_ffn_b[1])
    xp, cp1 = _ffn(xp, None, *fw, seq_len=L)
    xs, cs1 = _ffn(xs, state_ffn_conv[1], *fw, seq_len=t_len)

    heads = lambda r, lead: r.reshape(1, lead, -1, N_KV, HEAD_DIM)
    heads_t = lambda r: r.reshape(N_KV, HEAD_DIM, -1).transpose(2, 0, 1)[None, None]
    win = min(WINDOW, L)
    wb = cache_k_win.shape[2]
    keep = min(WINDOW, wb + t_len)
    win_s = lambda buf, new: jnp.concatenate([buf[0], new.reshape(B, t_len, N_KV, HEAD_DIM)], axis=1)[None, :, wb + t_len - keep:]
    return (xp[None], xs.reshape(B, t_len, D_MODEL),
            st_p[0][None], st_p[1][None], st_s[0][None], st_s[1][None],
            heads_t(rows_p[0]), heads_t(rows_p[1]), heads_t(rows_p[2]), heads_t(rows_p[3]),
            heads(rows_s[0], B), heads(rows_s[1], B), heads(rows_s[2], B), heads(rows_s[3], B),
            heads_t(rows_p[4][:, L - win:]), heads_t(rows_p[5][:, L - win:]),
            win_s(cache_k_win, rows_s[4]), win_s(cache_v_win, rows_s[5]),
            jnp.stack([cp0, cp1]), jnp.stack([cs0, cs1]))


def _s5_layer(xp, xs, h_re, h_im, a_re, a_im, log_dt, b_re, b_im, c_re, c_im, d_skip, w_glu, g, b):
    assert xp.shape[0] == 1
    mats = _s5_prep(a_re, a_im, log_dt, b_re, b_im, c_re, c_im)
    P = S5_STATE
    x2p = xp[0]
    yp, hfp = _s5_core_prompt(x2p, mats)
    op = _s5_glu_ln(x2p, yp, d_skip, w_glu, g, b)
    x2s = xs.reshape(-1, D_MODEL)
    ys, hfs = _s5_core_sample(xs, h_re, h_im, mats)
    os_ = _s5_glu_ln(x2s, ys, d_skip, w_glu, g, b)
    st_p = (hfp[None, :, :P], hfp[None, :, P:])
    hfs = hfs.transpose(1, 0, 2)
    st_s = (hfs[:, :, :P], hfs[:, :, P:])
    return op, os_, st_p, st_s
```

```python
import functools
import math

import jax
import jax.numpy as jnp
from jax import lax
from jax.experimental import pallas as pl
from jax.experimental.pallas import tpu as pltpu

F32 = jnp.float32
BF16 = jnp.bfloat16

D_MODEL = 1024
S5_GROUP = 16
S5_GROUPS = D_MODEL // S5_GROUP
S5_STATE = 64
S5_T = 16
N_HEADS = 16
HEAD_DIM = 64
N_KV = 4
HEADS_PER_KV = N_HEADS // N_KV
KV_DIM = N_KV * HEAD_DIM
CMP_LEN = 32
CMP_STRIDE = 16
CMP_HIDDEN = 2 * HEAD_DIM
SEL_BLOCK = 64
N_SEL = 16
WINDOW = 512
PAGE_SIZE = 128
FORCE = 1e6
D_FF = 2816
CONV_W = 3
FFN_CHUNK = 256
DEPTH = 2
ALPHA = (2.0 * DEPTH) ** 0.25
LN_EPS = 1e-5
NEG_BIG = -1e30

VMEM_LIMIT = 56 * 1024 * 1024


def _cparams(*sem):
    return pltpu.CompilerParams(dimension_semantics=sem, vmem_limit_bytes=VMEM_LIMIT)


def _layer_norm(r, g, b):
    mu = jnp.mean(r, -1, keepdims=True)
    c = r - mu
    var = jnp.mean(c * c, -1, keepdims=True)
    return c * lax.rsqrt(var + LN_EPS) * g + b


def _swap(x):
    return pltpu.roll(x, S5_STATE, axis=x.ndim - 1)


def _lo_mask(shape):
    return lax.broadcasted_iota(jnp.int32, shape, len(shape) - 1) < S5_STATE


def _cmul(x, w):
    ws = _swap(w)
    lo = _lo_mask(w.shape)
    wr = jnp.where(lo, w, ws)
    wi = jnp.where(lo, -ws, w)
    return x * wr + _swap(x) * wi


def _s5_prep_kernel(a_ref, dt_ref, bt_ref, btt_ref, ct_ref, m_ref, w_ref, v_ref, z_ref):
    P = S5_STATE
    T = S5_T
    a = a_ref[0]
    dt = jnp.exp(dt_ref[0])
    lo = _lo_mask(a.shape)
    a_sw = _swap(a)
    are = jnp.where(lo, a, a_sw)
    aim = jnp.where(lo, a_sw, a)
    mag = jnp.exp(are * dt)
    ang = aim * dt
    ab = mag * jnp.where(lo, jnp.cos(ang), jnp.sin(ang))
    ab_sw = _swap(ab)
    abr = jnp.where(lo, ab, ab_sw)
    abi = jnp.where(lo, ab_sw, ab)
    den = are * are + aim * aim
    nr = abr - 1.0
    f = jnp.where(lo, nr * are + abi * aim, abi * are - nr * aim) / den

    pw = [ab]
    for _ in range(4):
        pw.append(_cmul(pw[-1], pw[-1]))
    z_ref[0] = jnp.concatenate(
        [jnp.broadcast_to(pw[3], (4, 2 * P)), jnp.broadcast_to(pw[4], (4, 2 * P))], axis=0)

    rows = lax.broadcasted_iota(jnp.int32, (T * S5_GROUP, 2 * P), 0) // S5_GROUP
    e = (T - 1) - rows
    one = jnp.where(_lo_mask((T * S5_GROUP, 2 * P)), 1.0, 0.0).astype(F32)
    apow = one
    for b in range(4):
        apow = jnp.where(((e >> b) & 1) == 1, _cmul(apow, pw[b]), apow)
    af = _cmul(apow, f)
    w_ref[0] = _cmul(af, btt_ref[0])

    col = [jnp.transpose(jnp.broadcast_to(p, (2 * P, 2 * P))) for p in pw[:4]]
    lane_e = lax.broadcasted_iota(jnp.int32, (P, T * S5_GROUP), 1) // S5_GROUP

    def col_cmul(xr, xi, c):
        cr = jnp.concatenate([c[:P], c[:P]], axis=1)
        ci = jnp.concatenate([c[P:], c[P:]], axis=1)
        return xr * cr - xi * ci, xr * ci + xi * cr

    pr = jnp.ones((P, T * S5_GROUP), F32)
    pi = jnp.zeros((P, T * S5_GROUP), F32)
    for b in range(4):
        nr_, ni_ = col_cmul(pr, pi, col[b])
        sel = ((lane_e >> b) & 1) == 1
        pr = jnp.where(sel, nr_, pr)
        pi = jnp.where(sel, ni_, pi)
    ctr = ct_ref[0, 0]
    cti = ct_ref[0, 1]
    qr = pr * ctr - pi * cti
    qi = pr * cti + pi * ctr
    q1r, q1i = col_cmul(qr, qi, col[0])
    v_ref[0] = jnp.concatenate([q1r, -q1i], axis=0).astype(v_ref.dtype)

    fr = jnp.where(lo, f, _swap(f))[:, :P]
    fi = jnp.where(lo, _swap(f), f)[:, :P]
    btr = bt_ref[0, 0]
    bti = bt_ref[0, 1]
    bfr = btr * fr - bti * fi
    bfi = btr * fi + bti * fr
    hp = lax.Precision.HIGHEST
    kk = (jnp.dot(bfr, qr, precision=hp, preferred_element_type=F32)
          - jnp.dot(bfi, qi, precision=hp, preferred_element_type=F32))
    lane = lax.broadcasted_iota(jnp.int32, kk.shape, 1)
    for s in range(T):
        blk = kk if s == 0 else jnp.where(lane >= S5_GROUP * s, pltpu.roll(kk, S5_GROUP * s, axis=1), 0.0)
        m_ref[0, S5_GROUP * s:S5_GROUP * (s + 1), :] = blk.astype(m_ref.dtype)


def _s5_prep(a_re, a_im, log_dt, b_re, b_im, c_re, c_im):
    G, P, T = S5_GROUPS, S5_STATE, S5_T
    a_pk = jnp.concatenate([a_re, a_im], axis=-1).reshape(G, 1, 2 * P)
    dt_pk = jnp.broadcast_to(log_dt.reshape(G, 1, 1), (G, 1, 2 * P))
    bt = jnp.stack([b_re, b_im], axis=1).transpose(0, 1, 3, 2)
    btt = jnp.tile(jnp.concatenate([b_re, b_im], axis=1).transpose(0, 2, 1), (1, T, 1))
    ct = jnp.tile(jnp.stack([c_re, c_im], axis=1).transpose(0, 1, 3, 2), (1, 1, 1, T))
    n = T * S5_GROUP
    return pl.pallas_call(
        _s5_prep_kernel,
        grid=(G,),
        in_specs=[
            pl.BlockSpec((1, 1, 2 * P), lambda g: (g, 0, 0)),
            pl.BlockSpec((1, 1, 2 * P), lambda g: (g, 0, 0)),
            pl.BlockSpec((1, 2, S5_GROUP, P), lambda g: (g, 0, 0, 0)),
            pl.BlockSpec((1, n, 2 * P), lambda g: (g, 0, 0)),
            pl.BlockSpec((1, 2, P, n), lambda g: (g, 0, 0, 0)),
        ],
        out_specs=[
            pl.BlockSpec((1, n, n), lambda g: (g, 0, 0)),
            pl.BlockSpec((1, n, 2 * P), lambda g: (g, 0, 0)),
            pl.BlockSpec((1, 2 * P, n), lambda g: (g, 0, 0)),
            pl.BlockSpec((1, 8, 2 * P), lambda g: (g, 0, 0)),
        ],
        out_shape=[
            jax.ShapeDtypeStruct((G, n, n), BF16),
            jax.ShapeDtypeStruct((G, n, 2 * P), F32),
            jax.ShapeDtypeStruct((G, 2 * P, n), BF16),
            jax.ShapeDtypeStruct((G, 8, 2 * P), F32),
        ],
        compiler_params=_cparams("parallel"),
        name="s5_prep",
    )(a_pk, dt_pk, bt, btt, ct)


def _s5_scan_kernel(u_ref, m_ref, w_ref, v_ref, z_ref, y_ref, hf_ref):
    u = u_ref[0]
    C = u.shape[0]
    s = jnp.dot(u, w_ref[0].astype(BF16), preferred_element_type=F32)
    z = z_ref[0, 4:5, :]
    row = lax.broadcasted_iota(jnp.int32, s.shape, 0)
    h = s
    d = 1
    while d < C:
        hs = jnp.where(row >= d, pltpu.roll(h, d, axis=0), 0.0)
        h = h + _cmul(hs, z)
        z = _cmul(z, z)
        d *= 2
    hprev = jnp.where(row >= 1, pltpu.roll(h, 1, axis=0), 0.0)
    y = (jnp.dot(u, m_ref[0], preferred_element_type=F32)
         + jnp.dot(hprev.astype(BF16), v_ref[0], preferred_element_type=F32))
    y_ref[0] = y.astype(y_ref.dtype)
    hf_ref[0] = h[C - 8:, :]


def _s5_step_kernel(u_ref, h0_ref, m_ref, w_ref, v_ref, z_ref, y_ref, hf_ref, *, t_len):
    n = t_len * S5_GROUP
    off = (S5_T - t_len) * S5_GROUP
    u = u_ref[0]
    h0 = h0_ref[0]
    hp = lax.Precision.HIGHEST
    s = jnp.dot(u, w_ref[0, off:, :], precision=hp, preferred_element_type=F32)
    z = z_ref[0, 0:1, :] if t_len * 2 == S5_T else z_ref[0, 4:5, :]
    hf_ref[0] = _cmul(h0, z) + s
    y = (jnp.dot(u.astype(BF16), m_ref[0, :n, :n], preferred_element_type=F32)
         + jnp.dot(h0.astype(BF16), v_ref[0, :, :n], preferred_element_type=F32))
    y_ref[0] = y.astype(y_ref.dtype)


def _s5_core_prompt(x, mats):
    m, w, v, z = mats
    L = x.shape[0]
    G, T = S5_GROUPS, S5_T
    C = L // T
    n = T * S5_GROUP
    u = x.reshape(C, T, G, S5_GROUP).transpose(2, 0, 1, 3).reshape(G, C, n).astype(BF16)
    y, hf = pl.pallas_call(
        _s5_scan_kernel,
        grid=(G,),
        in_specs=[
            pl.BlockSpec((1, C, n), lambda g: (g, 0, 0)),
            pl.BlockSpec((1, n, n), lambda g: (g, 0, 0)),
            pl.BlockSpec((1, n, 2 * S5_STATE), lambda g: (g, 0, 0)),
            pl.BlockSpec((1, 2 * S5_STATE, n), lambda g: (g, 0, 0)),
            pl.BlockSpec((1, 8, 2 * S5_STATE), lambda g: (g, 0, 0)),
        ],
        out_specs=[
            pl.BlockSpec((1, C, n), lambda g: (g, 0, 0)),
            pl.BlockSpec((1, 8, 2 * S5_STATE), lambda g: (g, 0, 0)),
        ],
        out_shape=[
            jax.ShapeDtypeStruct((G, C, n), BF16),
            jax.ShapeDtypeStruct((G, 8, 2 * S5_STATE), F32),
        ],
        compiler_params=_cparams("parallel"),
        name="s5_scan",
    )(u, m, w, v, z)
    y = y.reshape(G, C, T, S5_GROUP).transpose(1, 2, 0, 3).reshape(L, D_MODEL)
    return y, hf[:, 7, :]


def _s5_core_sample(x, h_re, h_im, mats):
    m, w, v, z = mats
    B, t_len, _ = x.shape
    assert t_len in (S5_T // 2, S5_T)
    G = S5_GROUPS
    n = t_len * S5_GROUP
    u = x.reshape(B, t_len, G, S5_GROUP).transpose(2, 0, 1, 3).reshape(G, B, n)
    h0 = jnp.concatenate([h_re, h_im], axis=-1).transpose(1, 0, 2)
    nf = S5_T * S5_GROUP
    y, hf = pl.pallas_call(
        functools.partial(_s5_step_kernel, t_len=t_len),
        grid=(G,),
        in_specs=[
            pl.BlockSpec((1, B, n), lambda g: (g, 0, 0)),
            pl.BlockSpec((1, B, 2 * S5_STATE), lambda g: (g, 0, 0)),
            pl.BlockSpec((1, nf, nf), lambda g: (g, 0, 0)),
            pl.BlockSpec((1, nf, 2 * S5_STATE), lambda g: (g, 0, 0)),
            pl.BlockSpec((1, 2 * S5_STATE, nf), lambda g: (g, 0, 0)),
            pl.BlockSpec((1, 8, 2 * S5_STATE), lambda g: (g, 0, 0)),
        ],
        out_specs=[
            pl.BlockSpec((1, B, n), lambda g: (g, 0, 0)),
            pl.BlockSpec((1, B, 2 * S5_STATE), lambda g: (g, 0, 0)),
        ],
        out_shape=[
            jax.ShapeDtypeStruct((G, B, n), BF16),
            jax.ShapeDtypeStruct((G, B, 2 * S5_STATE), F32),
        ],
        compiler_params=_cparams("parallel"),
        name="s5_step",
    )(u, h0, m, w, v, z)
    y = y.reshape(G, B, t_len, S5_GROUP).transpose(1, 2, 0, 3).reshape(B * t_len, D_MODEL)
    return y, hf


def _s5_glu_ln_kernel(x_ref, y_ref, d_ref, w_ref, g_ref, b_ref, o_ref):
    x = x_ref[...]
    yy = y_ref[...].astype(F32) + x * d_ref[...]
    zg = jax.nn.gelu(yy).astype(BF16)
    zz = jnp.dot(zg, w_ref[...], preferred_element_type=F32)
    out = zz[:, :D_MODEL] * jax.nn.sigmoid(zz[:, D_MODEL:])
    o_ref[...] = _layer_norm(ALPHA * x + out, g_ref[...], b_ref[...])


def _row_tile(n, cap):
    t = min(n, cap)
    while n % t:
        t //= 2
    return t


def _s5_glu_ln(x, y, d_skip, w_glu, g, b):
    n = x.shape[0]
    tm = _row_tile(n, 512)
    const = lambda i: (0, 0)
    return pl.pallas_call(
        _s5_glu_ln_kernel,
        grid=(n // tm,),
        in_specs=[
            pl.BlockSpec((tm, D_MODEL), lambda i: (i, 0)),
            pl.BlockSpec((tm, D_MODEL), lambda i: (i, 0)),
            pl.BlockSpec((1, D_MODEL), const),
            pl.BlockSpec((D_MODEL, 2 * D_MODEL), const),
            pl.BlockSpec((1, D_MODEL), const),
            pl.BlockSpec((1, D_MODEL), const),
        ],
        out_specs=pl.BlockSpec((tm, D_MODEL), lambda i: (i, 0)),
        out_shape=jax.ShapeDtypeStruct((n, D_MODEL), F32),
        compiler_params=_cparams("parallel"),
        name="s5_glu_ln",
    )(x, y, d_skip.reshape(1, -1), w_glu.astype(BF16), g.reshape(1, -1), b.reshape(1, -1))


def _ffn_kernel(x_ref, p1_ref, p2_ref, wu_ref, cw_ref, cb_ref, wd_ref, g_ref, b_ref,
                o_ref, st_ref, carry_ref, *, period, chained):
    i = pl.program_id(0)
    x = x_ref[...]
    xb = x.astype(BF16)
    tm = x.shape[0]
    row = lax.broadcasted_iota(jnp.int32, (tm, 1), 0)
    t = row % period
    if chained:
        @pl.when(i == 0)
        def _():
            carry_ref[...] = jnp.zeros_like(carry_ref)

    def conv(cols):
        h = jnp.dot(xb, wu_ref[:, cols], preferred_element_type=F32)
        h1 = jnp.where(t >= 1, pltpu.roll(h, 1, axis=0), 0.0)
        h2 = jnp.where(t >= 2, pltpu.roll(h, 2, axis=0), 0.0)
        if chained:
            c0 = carry_ref[6:7, cols]
            c1 = carry_ref[7:8, cols]
            h1 = h1 + jnp.where(row == 0, c1, 0.0)
            h2 = h2 + jnp.where(row == 0, c0, 0.0) + jnp.where(row == 1, c1, 0.0)
            carry_ref[:, cols] = h[tm - 8:, :]
            st_ref[:, cols] = h[tm - 8:, :]
        else:
            h1 = h1 + p1_ref[:, cols]
            h2 = h2 + p2_ref[:, cols]
            st_ref[:, cols] = h
        return cb_ref[:, cols] + cw_ref[0:1, cols] * h2 + cw_ref[1:2, cols] * h1 + cw_ref[2:3, cols] * h

    f = jnp.zeros((tm, D_MODEL), F32)
    for c in range(D_FF // FFN_CHUNK):
        ca = slice(c * FFN_CHUNK, (c + 1) * FFN_CHUNK)
        cv = slice(D_FF + c * FFN_CHUNK, D_FF + (c + 1) * FFN_CHUNK)
        act = jax.nn.silu(conv(ca)) * conv(cv)
        f = f + jnp.dot(act.astype(BF16), wd_ref[ca, :], preferred_element_type=F32)
    o_ref[...] = _layer_norm(ALPHA * x + f, g_ref[...], b_ref[...])


def _ffn(x, buf, w_up, conv_w, conv_b, w_down, g, b, *, seq_len):
    n = x.shape[0]
    n_seq = n // seq_len
    f2 = 2 * D_FF
    chained = buf is None
    if chained:
        assert n_seq == 1
        tm = _row_tile(n, 512)
        p1 = p2 = jnp.zeros((8, f2), F32)
        pspec = pl.BlockSpec((8, f2), lambda i: (0, 0))
        st_shape, st_spec = (8, f2), pl.BlockSpec((8, f2), lambda i: (0, 0))
    else:
        tm = n
        z = jnp.zeros((n_seq, seq_len, f2), F32)
        p1 = z.at[:, 0].set(buf[:, 1]).reshape(n, f2)
        p2 = z.at[:, 0].set(buf[:, 0]).at[:, 1].set(buf[:, 1]).reshape(n, f2)
        pspec = pl.BlockSpec((tm, f2), lambda i: (i, 0))
        st_shape, st_spec = (n, f2), pl.BlockSpec((tm, f2), lambda i: (i, 0))
    const = lambda i: (0, 0)
    out, st = pl.pallas_call(
        functools.partial(_ffn_kernel, period=seq_len, chained=chained),
        grid=(n // tm,),
        in_specs=[
            pl.BlockSpec((tm, D_MODEL), lambda i: (i, 0)),
            pspec, pspec,
            pl.BlockSpec((D_MODEL, f2), const, pipeline_mode=pl.Buffered(1)),
            pl.BlockSpec((CONV_W, f2), const),
            pl.BlockSpec((1, f2), const),
            pl.BlockSpec((D_FF, D_MODEL), const, pipeline_mode=pl.Buffered(1)),
            pl.BlockSpec((1, D_MODEL), const),
            pl.BlockSpec((1, D_MODEL), const),
        ],
        out_specs=[pl.BlockSpec((tm, D_MODEL), lambda i: (i, 0)), st_spec],
        out_shape=[jax.ShapeDtypeStruct((n, D_MODEL), F32), jax.ShapeDtypeStruct(st_shape, F32)],
        scratch_shapes=[pltpu.VMEM((8, f2), F32)],
        compiler_params=_cparams("arbitrary"),
        name="conv_ffn",
    )(x, p1, p2, w_up.astype(BF16), conv_w, conv_b.reshape(1, -1), w_down.astype(BF16),
      g.reshape(1, -1), b.reshape(1, -1))
    if chained:
        state = st[6:8][None]
    else:
        state = st.reshape(n_seq, seq_len, f2)[:, seq_len - 2:]
    return out, state


Q_SCALE = HEAD_DIM ** -0.5 * math.log2(math.e)
TQ = 128
TK = 512


def _perm_heads():
    idx = jnp.arange(D_MODEL).reshape(N_KV, HEADS_PER_KV, HEAD_DIM)
    return idx.transpose(1, 0, 2).reshape(-1)


def _nsa_proj_kernel(x_ref, wq_ref, wkv_ref, wg_ref, *refs, rows_t):
    xb = x_ref[...].astype(BF16)
    q = jnp.dot(xb, wq_ref[...], preferred_element_type=F32) * Q_SCALE
    kv = jnp.dot(xb, wkv_ref[...], preferred_element_type=F32)
    gates = jax.nn.sigmoid(jnp.dot(xb, wg_ref[...], preferred_element_type=F32))
    if rows_t:
        wkvt_ref, q_ref, kc_ref, vc_ref, kvb_ref, g_ref = refs[:6]
        kvt = lax.dot_general(wkvt_ref[...], xb, (((1,), (1,)), ((), ())), preferred_element_type=F32)
        for j, r in enumerate(refs[6:]):
            r[...] = kvt[j * KV_DIM:(j + 1) * KV_DIM, :]
    else:
        q_ref, kc_ref, vc_ref, kvb_ref, g_ref = refs[:5]
        for j, r in enumerate(refs[5:]):
            r[...] = kv[:, (j + 2) * KV_DIM:(j + 3) * KV_DIM]
    q_ref[...] = q.astype(BF16)
    kc_ref[...] = kv[:, :KV_DIM]
    vc_ref[...] = kv[:, KV_DIM:2 * KV_DIM]
    kvb_ref[...] = kv[:, 2 * KV_DIM:].astype(BF16)
    g_ref[...] = gates.astype(BF16)


def _gate_columns():
    br = jnp.arange(3)[:, None, None, None]
    h = jnp.arange(HEADS_PER_KV)[None, :, None, None]
    g = jnp.arange(N_KV)[None, None, :, None]
    idx = (g * HEADS_PER_KV + h) * 3 + br + jnp.zeros((1, 1, 1, HEAD_DIM), jnp.int32)
    return idx.reshape(-1)


def _nsa_proj(x, w_in, *, rows_t):
    n = x.shape[0]
    tm = _row_tile(n, 512)
    hd = N_HEADS * HEAD_DIM
    wq = w_in[:, :hd][:, _perm_heads()].astype(BF16)
    wkv = w_in[:, hd:hd + 6 * KV_DIM].astype(BF16)
    wg = w_in[:, hd + 6 * KV_DIM:][:, _gate_columns()].astype(BF16)
    const = lambda i: (0, 0)
    rows = lambda w: pl.BlockSpec((tm, w), lambda i: (i, 0))
    in_specs = [rows(D_MODEL), pl.BlockSpec((D_MODEL, hd), const), pl.BlockSpec((D_MODEL, 6 * KV_DIM), const),
                pl.BlockSpec((D_MODEL, 3 * hd), const)]
    out_specs = [rows(hd), rows(KV_DIM), rows(KV_DIM), rows(4 * KV_DIM), rows(3 * hd)]
    out_shape = [jax.ShapeDtypeStruct((n, hd), BF16), jax.ShapeDtypeStruct((n, KV_DIM), F32),
                 jax.ShapeDtypeStruct((n, KV_DIM), F32), jax.ShapeDtypeStruct((n, 4 * KV_DIM), BF16),
                 jax.ShapeDtypeStruct((n, 3 * hd), BF16)]
    args = [x, wq, wkv, wg]
    if rows_t:
        in_specs.append(pl.BlockSpec((6 * KV_DIM, D_MODEL), const))
        args.append(wkv.T)
        out_specs += [pl.BlockSpec((KV_DIM, tm), lambda i: (0, i))] * 6
        out_shape += [jax.ShapeDtypeStruct((KV_DIM, n), F32)] * 6
    else:
        out_specs += [rows(KV_DIM)] * 4
        out_shape += [jax.ShapeDtypeStruct((n, KV_DIM), F32)] * 4
    return pl.pallas_call(
        functools.partial(_nsa_proj_kernel, rows_t=rows_t),
        grid=(n // tm,),
        in_specs=in_specs,
        out_specs=out_specs,
        out_shape=out_shape,
        compiler_params=_cparams("parallel"),
        name="nsa_proj",
    )(*args)


def _cmp_weights(pe, w1, b1, w2):
    eye = jnp.eye(N_KV, dtype=F32)
    w1b = jnp.einsum('jsdh,ge->jsgdeh', w1.reshape(2, CMP_STRIDE, HEAD_DIM, CMP_HIDDEN), eye)
    w1b = w1b.reshape(2, CMP_STRIDE * KV_DIM, N_KV * CMP_HIDDEN).astype(BF16)
    w2b = jnp.einsum('hd,ge->ghed', w2, eye).reshape(N_KV * CMP_HIDDEN, KV_DIM).astype(BF16)
    peb = jnp.broadcast_to(pe.reshape(2, CMP_STRIDE, 1, HEAD_DIM), (2, CMP_STRIDE, N_KV, HEAD_DIM))
    peb = peb.reshape(2, 1, CMP_STRIDE * KV_DIM)
    b1b = jnp.tile(b1, N_KV).reshape(1, N_KV * CMP_HIDDEN)
    return peb, w1b, b1b, w2b


def _compress_rows(x, carry_ref, pe_ref, w1_ref, b1_ref, w2_ref):
    p0 = jnp.dot((x + pe_ref[0]).astype(BF16), w1_ref[0], preferred_element_type=F32)
    p1 = jnp.dot((x + pe_ref[1]).astype(BF16), w1_ref[1], preferred_element_type=F32)
    row = lax.broadcasted_iota(jnp.int32, (x.shape[0], 1), 0)
    p0s = jnp.where(row == 0, carry_ref[7:8, :], pltpu.roll(p0, 1, axis=0))
    carry_ref[...] = p0[x.shape[0] - 8:, :]
    h = b1_ref[...] + p0s + p1
    return jnp.dot(jax.nn.gelu(h).astype(BF16), w2_ref[...], preferred_element_type=F32)


def _compress_prompt_kernel(x_ref, pe_ref, w1_ref, b1_ref, w2_ref, o_ref, carry_ref):
    @pl.when(pl.program_id(0) == 0)
    def _():
        carry_ref[...] = jnp.zeros_like(carry_ref)
    o_ref[...] = _compress_rows(x_ref[...], carry_ref, pe_ref, w1_ref, b1_ref, w2_ref).astype(o_ref.dtype)


def _compress_prompt(rows, cw):
    peb, w1b, b1b, w2b = cw
    n = rows.shape[0]
    nc = n // CMP_STRIDE
    wide = CMP_STRIDE * KV_DIM
    x = rows.reshape(nc, wide)
    tc = _row_tile(nc, 256)
    return pl.pallas_call(
        _compress_prompt_kernel,
        grid=(nc // tc,),
        in_specs=[pl.BlockSpec((tc, wide), lambda i: (i, 0)),
                  pl.BlockSpec((2, 1, wide), lambda i: (0, 0, 0)),
                  pl.BlockSpec((2, wide, N_KV * CMP_HIDDEN), lambda i: (0, 0, 0)),
                  pl.BlockSpec((1, N_KV * CMP_HIDDEN), lambda i: (0, 0)),
                  pl.BlockSpec((N_KV * CMP_HIDDEN, KV_DIM), lambda i: (0, 0))],
        out_specs=pl.BlockSpec((tc, KV_DIM), lambda i: (i, 0)),
        out_shape=jax.ShapeDtypeStruct((nc, KV_DIM), BF16),
        scratch_shapes=[pltpu.VMEM((8, N_KV * CMP_HIDDEN), F32)],
        compiler_params=_cparams("arbitrary"),
        name="nsa_compress",
    )(x, peb, w1b, b1b, w2b)


def _masked_softmax_rows(s, valid):
    s = jnp.where(valid, s, NEG_BIG)
    m = jnp.max(s, -1, keepdims=True)
    e = jnp.where(valid, jnp.exp2(s - m), 0.0)
    den = jnp.sum(e, -1, keepdims=True)
    return e / jnp.where(den > 0, den, 1.0)


def _biased_softmax_rows(s, bias, row_ok):
    s = s + bias
    m = jnp.max(s, -1, keepdims=True)
    e = jnp.exp2(s - m)
    den = jnp.sum(e, -1, keepdims=True)
    return e * jnp.where(row_ok, 1.0 / den, 0.0)


def _select_blocks(imp, cur, n_sel, axis=1):
    ns = imp.shape[axis]
    jj = lax.broadcasted_iota(jnp.int32, (1, ns) if axis == 1 else (ns, 1), axis)
    forced = (jj == 0) | ((jj <= cur) & (jj >= cur - 1))
    v = jnp.where(jj > cur, -FORCE, jnp.where(forced, FORCE, imp))
    sel = jnp.zeros(imp.shape, jnp.bool_)
    for _ in range(min(n_sel, ns)):
        m = jnp.max(v, axis, keepdims=True)
        idx = jnp.min(jnp.where(v == m, jj, ns), axis, keepdims=True)
        hit = jj == idx
        sel = sel | hit
        v = jnp.where(hit, -jnp.inf, v)
    return sel


def _nsa_attn_prompt_kernel(q_ref, gate_ref, ks_ref, vs_ref, *rest):
    nwb = WINDOW // TQ + 1
    kw_refs, vw_refs = rest[:nwb], rest[nwb:2 * nwb]
    kc_ref, vc_ref, c2st_ref, o_ref, qg_ref, selb_ref, m_ref, l_ref, acc_ref, oo_ref = rest[2 * nwb:]
    i = pl.program_id(0)
    nh = HEADS_PER_KV
    nc = kc_ref.shape[0]
    ns = c2st_ref.shape[0]
    rows = nh * TQ
    hd = N_HEADS * HEAD_DIM
    q = q_ref[...]
    qh = jnp.concatenate([q[:, KV_DIM * h:KV_DIM * (h + 1)] for h in range(nh)], axis=0)
    lane_g = lax.broadcasted_iota(jnp.int32, (1, KV_DIM), 1) // HEAD_DIM
    qpos_r = i * TQ + lax.broadcasted_iota(jnp.int32, (rows, 1), 0) % TQ
    qpos_q = i * TQ + lax.broadcasted_iota(jnp.int32, (TQ, 1), 0)
    cur_t = (i * TQ + lax.broadcasted_iota(jnp.int32, (1, TQ), 1)) // SEL_BLOCK
    nt = (((i + 1) * TQ + TK - 1) // TK)
    nt_dims = (((1,), (1,)), ((), ()))

    def gate_rows(br):
        return jnp.concatenate([gate_ref[:, br * hd + KV_DIM * h:br * hd + KV_DIM * (h + 1)] for h in range(nh)],
                               axis=0).astype(F32)

    kwin = jnp.concatenate([r[...] for r in kw_refs], axis=0)
    vwin = jnp.concatenate([r[...] for r in vw_refs], axis=0)
    kw_pos = i * TQ - WINDOW + lax.broadcasted_iota(jnp.int32, (1, WINDOW + TQ), 1)
    dlt = qpos_r - kw_pos
    bias_w = jnp.where((dlt >= 0) & (dlt < WINDOW) & (kw_pos >= 0), 0.0, NEG_BIG)
    n_idx = lax.broadcasted_iota(jnp.int32, (1, nc), 1)
    bias_c = jnp.where((n_idx >= 1) & (n_idx * CMP_STRIDE + (CMP_STRIDE - 1) <= qpos_r), 0.0, NEG_BIG)
    row_ok_c = qpos_r >= 2 * CMP_STRIDE - 1
    g_c, g_w = gate_rows(0), gate_rows(2)

    impt = []
    for g in range(N_KV):
        qg = jnp.where(lane_g == g, qh, jnp.zeros_like(qh))
        qg_ref[g] = qg
        s_c = lax.dot_general(qg, kc_ref[...], nt_dims, preferred_element_type=F32)
        p_c = _biased_softmax_rows(s_c, bias_c, row_ok_c)
        o_c = jnp.dot(p_c.astype(BF16), vc_ref[...], preferred_element_type=F32)
        pcs = p_c[0:TQ]
        for h in range(1, nh):
            pcs = pcs + p_c[h * TQ:(h + 1) * TQ]
        acc_i = jnp.zeros((ns, TQ), F32)
        rem = pcs
        for _ in range(3):
            part = rem.astype(BF16)
            rem = rem - part.astype(F32)
            acc_i = acc_i + lax.dot_general(c2st_ref[...], part, nt_dims, preferred_element_type=F32)
        impt.append(acc_i)
        s_w = lax.dot_general(qg, kwin, nt_dims, preferred_element_type=F32)
        p_w = _biased_softmax_rows(s_w, bias_w, True)
        o_w = jnp.dot(p_w.astype(BF16), vwin, preferred_element_type=F32)
        oo_ref[g] = g_c * o_c + g_w * o_w
        m_ref[g] = jnp.full((rows, 128), NEG_BIG, F32)
        l_ref[g] = jnp.zeros((rows, 128), F32)
        acc_ref[g] = jnp.zeros((rows, KV_DIM), F32)
    sel_t = _select_blocks(jnp.concatenate(impt, axis=1), jnp.concatenate([cur_t] * N_KV, axis=1), N_SEL, axis=0)
    sel_t = sel_t.astype(F32)
    for g in range(N_KV):
        selb_ref[g] = jnp.transpose(sel_t[:, g * TQ:(g + 1) * TQ]).astype(BF16)

    j_row = lax.broadcasted_iota(jnp.int32, (ns, TK), 0)
    j_of_lane = lax.broadcasted_iota(jnp.int32, (ns, TK), 1) // SEL_BLOCK
    key_lane = lax.broadcasted_iota(jnp.int32, (1, TK), 1)

    def body(kt, carry):
        k0 = pl.multiple_of(kt * TK, TK)
        kblk = ks_ref[pl.ds(k0, TK), :]
        vblk = vs_ref[pl.ds(k0, TK), :]
        onehot = (j_row == kt * (TK // SEL_BLOCK) + j_of_lane).astype(BF16)
        causal = k0 + key_lane <= qpos_q
        for g in range(N_KV):
            picked = jnp.dot(selb_ref[g], onehot, preferred_element_type=F32)
            bias = jnp.where((picked > 0.5) & causal, 0.0, NEG_BIG)
            s = lax.dot_general(qg_ref[g], kblk, nt_dims, preferred_element_type=F32)
            s = s + jnp.concatenate([bias] * nh, axis=0)
            m_prev = m_ref[g]
            m_new = jnp.maximum(m_prev, jnp.max(s, -1, keepdims=True))
            alpha = jnp.exp2(m_prev - m_new)
            p = jnp.exp2(s - jnp.concatenate([m_new] * (TK // 128), axis=1))
            l_ref[g] = alpha * l_ref[g] + jnp.sum(p, -1, keepdims=True)
            acc_ref[g] = (jnp.concatenate([alpha] * (KV_DIM // 128), axis=1) * acc_ref[g]
                          + jnp.dot(p.astype(BF16), vblk, preferred_element_type=F32))
            m_ref[g] = m_new
        return carry

    lax.fori_loop(0, (nt + 1) // 2, lambda kk, c: body(2 * kk + 1, body(2 * kk, c)), 0)

    g_s = gate_rows(1)
    total = jnp.zeros((rows, KV_DIM), F32)
    for g in range(N_KV):
        o_s = acc_ref[g] / jnp.concatenate([l_ref[g]] * (KV_DIM // 128), axis=1)
        total = jnp.where(lane_g == g, oo_ref[g] + g_s * o_s, total)
    for h in range(nh):
        o_ref[:, KV_DIM * h:KV_DIM * (h + 1)] = total[h * TQ:(h + 1) * TQ].astype(o_ref.dtype)


def _cmp_to_sel(nc, ns):
    blk = jnp.arange(nc)[:, None] - 1
    i = blk * CMP_STRIDE
    j = jnp.arange(ns)[None, :] * SEL_BLOCK
    return ((blk >= 0) & (i < j + SEL_BLOCK) & (i + CMP_LEN > j)).astype(F32)


def _nsa_attn_prompt(q, gates, kvb, kc, vc):
    n = q.shape[0]
    nc, ns = n // CMP_STRIDE, n // SEL_BLOCK
    assert n % TK == 0 and WINDOW % TQ == 0
    wpad = jnp.pad(kvb[:, 2 * KV_DIM:], ((WINDOW, 0), (0, 0)))
    c2st = _cmp_to_sel(nc, ns).T.astype(BF16)
    nwb = WINDOW // TQ + 1
    rows = HEADS_PER_KV * TQ
    const2 = lambda i: (0, 0)
    one = pl.Buffered(1)
    in_specs = [
        pl.BlockSpec((TQ, D_MODEL), lambda i: (i, 0)),
        pl.BlockSpec((TQ, 3 * D_MODEL), lambda i: (i, 0)),
        pl.BlockSpec((n, KV_DIM), lambda i: (0, 0), pipeline_mode=one),
        pl.BlockSpec((n, KV_DIM), lambda i: (0, 1), pipeline_mode=one),
    ]
    in_specs += [pl.BlockSpec((TQ, KV_DIM), functools.partial(lambda i, k: (i + k, 0), k=k)) for k in range(nwb)]
    in_specs += [pl.BlockSpec((TQ, KV_DIM), functools.partial(lambda i, k: (i + k, 1), k=k)) for k in range(nwb)]
    in_specs += [
        pl.BlockSpec((nc, KV_DIM), const2, pipeline_mode=one),
        pl.BlockSpec((nc, KV_DIM), const2, pipeline_mode=one),
        pl.BlockSpec((ns, nc), const2, pipeline_mode=one),
    ]
    return pl.pallas_call(
        _nsa_attn_prompt_kernel,
        grid=(n // TQ,),
        in_specs=in_specs,
        out_specs=pl.BlockSpec((TQ, D_MODEL), lambda i: (i, 0)),
        out_shape=jax.ShapeDtypeStruct((n, D_MODEL), BF16),
        scratch_shapes=[pltpu.VMEM((N_KV, rows, KV_DIM), BF16), pltpu.VMEM((N_KV, TQ, ns), BF16),
                        pltpu.VMEM((N_KV, rows, 128), F32), pltpu.VMEM((N_KV, rows, 128), F32),
                        pltpu.VMEM((N_KV, rows, KV_DIM), F32), pltpu.VMEM((N_KV, rows, KV_DIM), F32)],
        compiler_params=_cparams("parallel"),
        name="nsa_attn_prompt",
    )(q, gates, kvb, kvb, *([wpad] * (2 * nwb)), kc, vc, c2st)


def _out_proj_ln_kernel(x_ref, o_ref, w_ref, g_ref, b_ref, y_ref):
    y = jnp.dot(o_ref[...], w_ref[...], preferred_element_type=F32)
    y_ref[...] = _layer_norm(ALPHA * x_ref[...] + y, g_ref[...], b_ref[...])


def _out_proj_ln(x, o, w_o, g, b):
    n = x.shape[0]
    tm = _row_tile(n, 512)
    const = lambda i: (0, 0)
    return pl.pallas_call(
        _out_proj_ln_kernel,
        grid=(n // tm,),
        in_specs=[pl.BlockSpec((tm, D_MODEL), lambda i: (i, 0)), pl.BlockSpec((tm, D_MODEL), lambda i: (i, 0)),
                  pl.BlockSpec((D_MODEL, D_MODEL), const), pl.BlockSpec((1, D_MODEL), const),
                  pl.BlockSpec((1, D_MODEL), const)],
        out_specs=pl.BlockSpec((tm, D_MODEL), lambda i: (i, 0)),
        out_shape=jax.ShapeDtypeStruct((n, D_MODEL), F32),
        compiler_params=_cparams("parallel"),
        name="nsa_out_ln",
    )(x, o, w_o[_perm_heads(), :].astype(BF16), g.reshape(1, -1), b.reshape(1, -1))


def _nsa_prompt_layer(x, w_in, pe, w1, b1, w2, w_o, g, b):
    q, k_c, v_c, kvb, gates, *rows_t = _nsa_proj(x, w_in, rows_t=True)
    kc = _compress_prompt(k_c, _cmp_weights(pe[0], w1[0], b1[0], w2[0]))
    vc = _compress_prompt(v_c, _cmp_weights(pe[1], w1[1], b1[1], w2[1]))
    o = _nsa_attn_prompt(q, gates, kvb, kc, vc)
    y = _out_proj_ln(x, o, w_o, g, b)
    return y, tuple(rows_t)


def _compress_sample_kernel(pt_ref, *refs, n_pg):
    pages = refs[:n_pg]
    new_ref, pe_ref, w1_ref, b1_ref, w2_ref, o_ref, xs_ref, carry_ref = refs[n_pg:]
    s = pl.program_id(1)
    last = pl.num_programs(1) - 1
    tc = n_pg * (PAGE_SIZE // CMP_STRIDE)

    @pl.when(s == 0)
    def _():
        carry_ref[...] = jnp.zeros_like(carry_ref)
    nl = KV_DIM // 128
    for k, p in enumerate(pages):
        xt = jnp.transpose(p[0])
        for c in range(nl):
            xs_ref[c, k * PAGE_SIZE:(k + 1) * PAGE_SIZE, :] = xt[:, c * 128:(c + 1) * 128]

    @pl.when(s == last)
    def _():
        for c in range(nl):
            xs_ref[c, 0:CMP_STRIDE, :] = new_ref[0, :, c * 128:(c + 1) * 128]
    nh = N_KV * CMP_HIDDEN
    p0 = jnp.zeros((tc, nh), F32)
    p1 = jnp.zeros((tc, nh), F32)
    for ss in range(CMP_STRIDE):
        x = jnp.concatenate([xs_ref[c, pl.ds(ss, tc, stride=CMP_STRIDE), :] for c in range(nl)], axis=1)
        p0 = p0 + jnp.dot((x + pe_ref[0, ss:ss + 1, :]).astype(BF16), w1_ref[0, ss], preferred_element_type=F32)
        p1 = p1 + jnp.dot((x + pe_ref[1, ss:ss + 1, :]).astype(BF16), w1_ref[1, ss], preferred_element_type=F32)
    row = lax.broadcasted_iota(jnp.int32, (tc, 1), 0)
    p0s = jnp.where(row == 0, carry_ref[7:8, :], pltpu.roll(p0, 1, axis=0))
    carry_ref[...] = p0[tc - 8:, :]
    h = b1_ref[...] + p0s + p1
    o_ref[0] = jnp.dot(jax.nn.gelu(h).astype(BF16), w2_ref[...], preferred_element_type=F32).astype(o_ref.dtype)


def _compress_sample(pool_t, page_table, new_rows, cw):
    peb, w1b, b1b, w2b = cw
    B, n_pages = page_table.shape
    t = new_rows.shape[1]
    assert t <= CMP_STRIDE
    cpp = PAGE_SIZE // CMP_STRIDE
    n_pg = min(16, n_pages)
    assert n_pages % n_pg == 0
    steps = n_pages // n_pg + 1
    nh = N_KV * CMP_HIDDEN
    pe_s = peb.reshape(2, CMP_STRIDE, KV_DIM)
    w1_s = w1b.reshape(2, CMP_STRIDE, KV_DIM, nh)
    newc = jnp.pad(new_rows, ((0, 0), (0, CMP_STRIDE - t), (0, 0)))

    def page_map(b, s, pt, k):
        return (pt[b * n_pages + jnp.minimum(s * n_pg + k, n_pages - 1)], 0, 0)

    tc = n_pg * cpp
    grid_spec = pltpu.PrefetchScalarGridSpec(
        num_scalar_prefetch=1,
        grid=(B, steps),
        in_specs=[pl.BlockSpec((1, KV_DIM, PAGE_SIZE), functools.partial(page_map, k=k)) for k in range(n_pg)] + [
            pl.BlockSpec((1, CMP_STRIDE, KV_DIM), lambda b, s, pt: (b, 0, 0)),
            pl.BlockSpec((2, CMP_STRIDE, KV_DIM), lambda b, s, pt: (0, 0, 0)),
            pl.BlockSpec((2, CMP_STRIDE, KV_DIM, nh), lambda b, s, pt: (0, 0, 0, 0)),
            pl.BlockSpec((1, nh), lambda b, s, pt: (0, 0)),
            pl.BlockSpec((nh, KV_DIM), lambda b, s, pt: (0, 0))],
        out_specs=pl.BlockSpec((1, tc, KV_DIM), lambda b, s, pt: (b, s, 0)),
        scratch_shapes=[pltpu.VMEM((KV_DIM // 128, n_pg * PAGE_SIZE, 128), F32), pltpu.VMEM((8, nh), F32)],
    )
    return pl.pallas_call(
        functools.partial(_compress_sample_kernel, n_pg=n_pg),
        grid_spec=grid_spec,
        out_shape=jax.ShapeDtypeStruct((B, steps * tc, KV_DIM), BF16),
        compiler_params=_cparams("parallel", "arbitrary"),
        name="nsa_compress_paged",
    )(page_table.reshape(-1), *([pool_t] * n_pg), newc, pe_s, w1_s, b1b, w2b)


def _nsa_attn_sample_kernel(pt_ref, q_ref, gate_ref, new_ref, *refs, n_pg, past, t_len, n_blk):
    kpages, vpages = refs[:n_pg], refs[n_pg:2 * n_pg]
    kwb_ref, vwb_ref, kc_ref, vc_ref, c2s_ref, o_ref, qs_ref, selb_ref, m_ref, l_ref, acc_ref, oo_ref = refs[2 * n_pg:]
    s = pl.program_id(1)
    nh = HEADS_PER_KV
    rows = nh * N_KV * t_len
    nc = kc_ref.shape[1]
    nsp = c2s_ref.shape[1]
    hp = lax.Precision.HIGHEST
    nt_dims = (((1,), (1,)), ((), ()))
    r_iota = lax.broadcasted_iota(jnp.int32, (rows, 1), 0)
    qpos_r = past + r_iota % t_len
    lane_g = lax.broadcasted_iota(jnp.int32, (1, KV_DIM), 1) // HEAD_DIM
    row_g = (r_iota // t_len) % N_KV

    def tile_rows(x, k):
        return jnp.concatenate([x] * k, axis=0)

    @pl.when(s == 0)
    def _():
        qf = q_ref[...].astype(F32)
        q128 = jnp.concatenate([tile_rows(qf[:, KV_DIM * h:KV_DIM * (h + 1)], N_KV) for h in range(nh)], axis=0)
        q128 = jnp.where(lane_g == row_g, q128, 0.0).astype(BF16)
        qs_ref[...] = q128
        hd = N_HEADS * HEAD_DIM
        gx = [jnp.concatenate([tile_rows(gate_ref[:, br * hd + KV_DIM * h:br * hd + KV_DIM * (h + 1)].astype(F32), N_KV)
                               for h in range(nh)], axis=0) for br in range(3)]
        new = new_ref[...]
        n_idx = lax.broadcasted_iota(jnp.int32, (1, nc), 1)
        cmp_ok = (n_idx >= 1) & (n_idx <= n_blk) & (n_idx * CMP_STRIDE + (CMP_STRIDE - 1) <= qpos_r)
        s_c = lax.dot_general(q128, kc_ref[0], nt_dims, preferred_element_type=F32)
        p_c = _masked_softmax_rows(s_c, cmp_ok)
        o_c = jnp.dot(p_c.astype(BF16), vc_ref[0], preferred_element_type=F32)
        gq = N_KV * t_len
        pcs = p_c[0:gq]
        for h in range(1, nh):
            pcs = pcs + p_c[h * gq:(h + 1) * gq]
        imp = jnp.dot(pcs, c2s_ref[...], precision=hp, preferred_element_type=F32)
        cur = (past + lax.broadcasted_iota(jnp.int32, (gq, 1), 0) % t_len) // SEL_BLOCK
        sel = _select_blocks(imp, cur, N_SEL).astype(F32)
        selb_ref[...] = tile_rows(sel, nh).astype(BF16)
        kwin = jnp.concatenate([kwb_ref[0], new[:, 2 * KV_DIM:3 * KV_DIM]], axis=0).astype(BF16)
        vwin = jnp.concatenate([vwb_ref[0], new[:, 3 * KV_DIM:]], axis=0).astype(BF16)
        wb = kwb_ref.shape[1]
        kw_pos = past - wb + lax.broadcasted_iota(jnp.int32, (1, wb + t_len), 1)
        dlt = qpos_r - kw_pos
        win_ok = (dlt >= 0) & (dlt < WINDOW) & (kw_pos >= 0)
        s_w = lax.dot_general(q128, kwin, nt_dims, preferred_element_type=F32)
        p_w = _masked_softmax_rows(s_w, win_ok)
        o_w = jnp.dot(p_w.astype(BF16), vwin, preferred_element_type=F32)
        oo_ref[0] = gx[0] * o_c + gx[2] * o_w
        oo_ref[1] = gx[1]
        knew = new[:, 0:KV_DIM].astype(BF16)
        vnew = new[:, KV_DIM:2 * KV_DIM].astype(BF16)
        s_n = lax.dot_general(q128, knew, nt_dims, preferred_element_type=F32)
        jn = lax.broadcasted_iota(jnp.int32, (nsp, t_len), 0)
        picked = jnp.dot(selb_ref[...], (jn == past // SEL_BLOCK).astype(BF16), preferred_element_type=F32)
        ok = (picked > 0.5) & (past + lax.broadcasted_iota(jnp.int32, (1, t_len), 1) <= qpos_r)
        s_n = jnp.where(ok, s_n, NEG_BIG)
        m0 = jnp.max(s_n, -1, keepdims=True)
        p0 = jnp.where(ok, jnp.exp2(s_n - m0), 0.0)
        m_ref[...] = m0
        l_ref[...] = jnp.sum(p0, -1, keepdims=True)
        acc_ref[...] = jnp.dot(p0.astype(BF16), vnew, preferred_element_type=F32)

    tk = n_pg * PAGE_SIZE
    qs = qs_ref[...]
    sc = jnp.concatenate([jnp.dot(qs, p[0].astype(BF16), preferred_element_type=F32) for p in kpages], axis=1)
    j_row = lax.broadcasted_iota(jnp.int32, (nsp, tk), 0)
    j_of_lane = lax.broadcasted_iota(jnp.int32, (nsp, tk), 1) // SEL_BLOCK
    onehot = (j_row == s * (tk // SEL_BLOCK) + j_of_lane).astype(BF16)
    picked = jnp.dot(selb_ref[...], onehot, preferred_element_type=F32)
    sc = jnp.where(picked > 0.5, sc, NEG_BIG)
    m = m_ref[...]
    m_new = jnp.maximum(m, jnp.max(sc, -1, keepdims=True))
    alpha = jnp.exp2(m - m_new)
    p = jnp.where(picked > 0.5, jnp.exp2(sc - m_new), 0.0)
    l_ref[...] = alpha * l_ref[...] + jnp.sum(p, -1, keepdims=True)
    pb = p.astype(BF16)
    pv = jnp.zeros((rows, KV_DIM), F32)
    for k, vp in enumerate(vpages):
        pv = pv + lax.dot_general(pb[:, k * PAGE_SIZE:(k + 1) * PAGE_SIZE], vp[0].astype(BF16), nt_dims,
                                  preferred_element_type=F32)
    acc_ref[...] = alpha * acc_ref[...] + pv
    m_ref[...] = m_new

    @pl.when(s == pl.num_programs(1) - 1)
    def _():
        l = l_ref[...]
        o_s = acc_ref[...] / jnp.where(l > 0, l, 1.0)
        total = oo_ref[0] + oo_ref[1] * o_s
        gq = N_KV * t_len
        for h in range(nh):
            oh = jnp.zeros((t_len, KV_DIM), F32)
            for g in range(N_KV):
                oh = jnp.where(lane_g == g, total[h * gq + g * t_len:h * gq + (g + 1) * t_len], oh)
            o_ref[:, KV_DIM * h:KV_DIM * (h + 1)] = oh.astype(o_ref.dtype)


def _nsa_attn_sample(q, gates, new, page_table, pool_ks, pool_vs, buf_kw, buf_vw, kc, vc, *, t_len):
    B, n_pages = page_table.shape
    past = n_pages * PAGE_SIZE
    total = past + t_len
    n_blk = -(-total // CMP_STRIDE) - 1
    ns = -(-total // SEL_BLOCK)
    nsp = -(-ns // 128) * 128
    nc = kc.shape[1]
    c2s = jnp.pad(_cmp_to_sel(nc, ns), ((0, 0), (0, nsp - ns)))
    n_pg = min(8, n_pages)
    assert n_pages % n_pg == 0 and t_len % 8 == 0
    steps = n_pages // n_pg
    rows = N_HEADS * t_len
    wb = buf_kw.shape[1]
    one = pl.Buffered(1)

    def page_map(b, s, pt, k):
        return (pt[b * n_pages + s * n_pg + k], 0, 0)

    per_b = lambda w: pl.BlockSpec((t_len, w), lambda b, s, pt: (b, 0))
    grid_spec = pltpu.PrefetchScalarGridSpec(
        num_scalar_prefetch=1,
        grid=(B, steps),
        in_specs=[per_b(D_MODEL), per_b(3 * D_MODEL), per_b(4 * KV_DIM)]
        + [pl.BlockSpec((1, KV_DIM, PAGE_SIZE), functools.partial(page_map, k=k)) for k in range(n_pg)] * 2
        + [pl.BlockSpec((1, wb, KV_DIM), lambda b, s, pt: (b, 0, 0)),
           pl.BlockSpec((1, wb, KV_DIM), lambda b, s, pt: (b, 0, 0)),
           pl.BlockSpec((1, nc, KV_DIM), lambda b, s, pt: (b, 0, 0)),
           pl.BlockSpec((1, nc, KV_DIM), lambda b, s, pt: (b, 0, 0)),
           pl.BlockSpec((nc, nsp), lambda b, s, pt: (0, 0), pipeline_mode=one)],
        out_specs=pl.BlockSpec((t_len, D_MODEL), lambda b, s, pt: (b, 0)),
        scratch_shapes=[pltpu.VMEM((rows, KV_DIM), BF16), pltpu.VMEM((rows, nsp), BF16),
                        pltpu.VMEM((rows, 1), F32), pltpu.VMEM((rows, 1), F32), pltpu.VMEM((rows, KV_DIM), F32),
                        pltpu.VMEM((2, rows, KV_DIM), F32)],
    )
    return pl.pallas_call(
        functools.partial(_nsa_attn_sample_kernel, n_pg=n_pg, past=past, t_len=t_len, n_blk=n_blk),
        grid_spec=grid_spec,
        out_shape=jax.ShapeDtypeStruct((B * t_len, D_MODEL), BF16),
        compiler_params=_cparams("parallel", "arbitrary"),
        name="nsa_attn_sample",
    )(page_table.reshape(-1), q, gates, new, *([pool_ks] * n_pg), *([pool_vs] * n_pg), buf_kw, buf_vw, kc, vc, c2s)


def _nsa_sample_layer(x, page_table, pool_kc, pool_vc, pool_ks, pool_vs, buf_kw, buf_vw,
                      w_in, pe, w1, b1, w2, w_o, g, b, *, t_len):
    B = page_table.shape[0]
    q, k_c, v_c, _, gates, k_s, v_s, k_w, v_w = _nsa_proj(x, w_in, rows_t=False)
    flat = lambda p: p.reshape(p.shape[0], p.shape[1], KV_DIM)
    pages_t = lambda p: p.transpose(0, 2, 3, 1).reshape(p.shape[0], KV_DIM, p.shape[1])
    kc = _compress_sample(pages_t(pool_kc), page_table, k_c.reshape(B, t_len, KV_DIM), _cmp_weights(pe[0], w1[0], b1[0], w2[0]))
    vc = _compress_sample(pages_t(pool_vc), page_table, v_c.reshape(B, t_len, KV_DIM), _cmp_weights(pe[1], w1[1], b1[1], w2[1]))
    new = jnp.concatenate([k_s, v_s, k_w, v_w], axis=1)
    o = _nsa_attn_sample(q, gates, new, page_table, pages_t(pool_ks), pages_t(pool_vs), flat(buf_kw), flat(buf_vw),
                         kc, vc, t_len=t_len)
    y = _out_proj_ln(x, o, w_o, g, b)
    return y, (k_c, v_c, k_s, v_s, k_w, v_w), o


def kernel(x_prompt, x_sample, state_s5_re, state_s5_im, cache_k_cmp, cache_v_cmp, cache_k_sel, cache_v_sel,
           cache_k_win, cache_v_win, state_ffn_conv, page_table,
           s5_a_re, s5_a_im, s5_log_dt, s5_b_re, s5_b_im, s5_c_re, s5_c_im, s5_d, s5_w_glu,
           nsa_w_in, nsa_cmp_pe, nsa_cmp_w1, nsa_cmp_b1, nsa_cmp_w2, nsa_w_o,
           ffn_w_up, ffn_conv_w, ffn_conv_b, ffn_w_down,
           ln_mix_g, ln_mix_b, ln_ffn_g, ln_ffn_b):
    assert s5_a_re.shape[0] == 1 and nsa_w_in.shape[0] == 1 and ffn_w_up.shape[0] == DEPTH
    bsz, L, _ = x_prompt.shape
    B, t_len, _ = x_sample.shape
    xp, xs, st_p, st_s = _s5_layer(x_prompt, x_sample, state_s5_re[0], state_s5_im[0], s5_a_re[0], s5_a_im[0],
                                   s5_log_dt[0], s5_b_re[0], s5_b_im[0], s5_c_re[0], s5_c_im[0], s5_d[0],
                                   s5_w_glu[0], ln_mix_g[0], ln_mix_b[0])
    fw = (ffn_w_up[0], ffn_conv_w[0], ffn_conv_b[0], ffn_w_down[0], ln_ffn_g[0], ln_ffn_b[0])
    xp, cp0 = _ffn(xp, None, *fw, seq_len=L)
    xs, cs0 = _ffn(xs, state_ffn_conv[0], *fw, seq_len=t_len)
    nw = (nsa_w_in[0], nsa_cmp_pe[0], nsa_cmp_w1[0], nsa_cmp_b1[0], nsa_cmp_w2[0], nsa_w_o[0],
          ln_mix_g[1], ln_mix_b[1])
    xp, rows_p = _nsa_prompt_layer(xp, *nw)
    xs, rows_s, _ = _nsa_sample_layer(xs, page_table, cache_k_cmp[0], cache_v_cmp[0], cache_k_sel[0], cache_v_sel[0],
                                      cache_k_win[0], cache_v_win[0], *nw, t_len=t_len)
    fw = (ffn_w_up[1], ffn_conv_w[1], ffn_conv_b[1], ffn_w_down[1], ln_ffn_g[1], ln_ffn_b[1])
    xp, cp1 = _ffn(xp, None, *fw, seq_len=L)
    xs, cs1 = _ffn(xs, state_ffn_conv[1], *fw, seq_len=t_len)

    heads = lambda r, lead: r.reshape(1, lead, -1, N_KV, HEAD_DIM)
    heads_t = lambda r: r.reshape(N_KV, HEAD_DIM, -1).transpose(2, 0, 1)[None, None]
    win = min(WINDOW, L)
    wb = cache_k_win.shape[2]
    keep = min(WINDOW, wb + t_len)
    win_s = lambda buf, new: jnp.concatenate([buf[0], new.reshape(B, t_len, N_KV, HEAD_DIM)], axis=1)[None, :, wb + t_len - keep:]
    return (xp[None], xs.reshape(B, t_len, D_MODEL),
            st_p[0][None], st_p[1][None], st_s[0][None], st_s[1][None],
            heads_t(rows_p[0]), heads_t(rows_p[1]), heads_t(rows_p[2]), heads_t(rows_p[3]),
            heads(rows_s[0], B), heads(rows_s[1], B), heads(rows_s[2], B), heads(rows_s[3], B),
            heads_t(rows_p[4][:, L - win:]), heads_t(rows_p[5][:, L - win:]),
            win_s(cache_k_win, rows_s[4]), win_s(cache_v_win, rows_s[5]),
            jnp.stack([cp0, cp1]), jnp.stack([cs0, cs1]))


def _s5_layer(xp, xs, h_re, h_im, a_re, a_im, log_dt, b_re, b_im, c_re, c_im, d_skip, w_glu, g, b):
    assert xp.shape[0] == 1
    mats = _s5_prep(a_re, a_im, log_dt, b_re, b_im, c_re, c_im)
    P = S5_STATE
    x2p = xp[0]
    yp, hfp = _s5_core_prompt(x2p, mats)
    op = _s5_glu_ln(x2p, yp, d_skip, w_glu, g, b)
    x2s = xs.reshape(-1, D_MODEL)
    ys, hfs = _s5_core_sample(xs, h_re, h_im, mats)
    os_ = _s5_glu_ln(x2s, ys, d_skip, w_glu, g, b)
    st_p = (hfp[None, :, :P], hfp[None, :, P:])
    hfs = hfs.transpose(1, 0, 2)
    st_s = (hfs[:, :, :P], hfs[:, :, P:])
    return op, os_, st_p, st_s
```

```python
import functools
import math

import jax
import jax.numpy as jnp
from jax import lax
from jax.experimental import pallas as pl
from jax.experimental.pallas import tpu as pltpu

F32 = jnp.float32
BF16 = jnp.bfloat16

D_MODEL = 1024
S5_GROUP = 16
S5_GROUPS = D_MODEL // S5_GROUP
S5_STATE = 64
S5_T = 16
N_HEADS = 16
HEAD_DIM = 64
N_KV = 4
HEADS_PER_KV = N_HEADS // N_KV
KV_DIM = N_KV * HEAD_DIM
CMP_LEN = 32
CMP_STRIDE = 16
CMP_HIDDEN = 2 * HEAD_DIM
SEL_BLOCK = 64
N_SEL = 16
WINDOW = 512
PAGE_SIZE = 128
FORCE = 1e6
D_FF = 2816
CONV_W = 3
FFN_CHUNK = 256
DEPTH = 2
ALPHA = (2.0 * DEPTH) ** 0.25
LN_EPS = 1e-5
NEG_BIG = -1e30

VMEM_LIMIT = 56 * 1024 * 1024


def _cparams(*sem):
    return pltpu.CompilerParams(dimension_semantics=sem, vmem_limit_bytes=VMEM_LIMIT)


def _layer_norm(r, g, b):
    mu = jnp.mean(r, -1, keepdims=True)
    c = r - mu
    var = jnp.mean(c * c, -1, keepdims=True)
    return c * lax.rsqrt(var + LN_EPS) * g + b


def _swap(x):
    return pltpu.roll(x, S5_STATE, axis=x.ndim - 1)


def _lo_mask(shape):
    return lax.broadcasted_iota(jnp.int32, shape, len(shape) - 1) < S5_STATE


def _cmul(x, w):
    ws = _swap(w)
    lo = _lo_mask(w.shape)
    wr = jnp.where(lo, w, ws)
    wi = jnp.where(lo, -ws, w)
    return x * wr + _swap(x) * wi


def _s5_prep_kernel(a_ref, dt_ref, bt_ref, btt_ref, ct_ref, m_ref, w_ref, v_ref, z_ref):
    P = S5_STATE
    T = S5_T
    a = a_ref[0]
    dt = jnp.exp(dt_ref[0])
    lo = _lo_mask(a.shape)
    a_sw = _swap(a)
    are = jnp.where(lo, a, a_sw)
    aim = jnp.where(lo, a_sw, a)
    mag = jnp.exp(are * dt)
    ang = aim * dt
    ab = mag * jnp.where(lo, jnp.cos(ang), jnp.sin(ang))
    ab_sw = _swap(ab)
    abr = jnp.where(lo, ab, ab_sw)
    abi = jnp.where(lo, ab_sw, ab)
    den = are * are + aim * aim
    nr = abr - 1.0
    f = jnp.where(lo, nr * are + abi * aim, abi * are - nr * aim) / den

    pw = [ab]
    for _ in range(4):
        pw.append(_cmul(pw[-1], pw[-1]))
    z_ref[0] = jnp.concatenate(
        [jnp.broadcast_to(pw[3], (4, 2 * P)), jnp.broadcast_to(pw[4], (4, 2 * P))], axis=0)

    rows = lax.broadcasted_iota(jnp.int32, (T * S5_GROUP, 2 * P), 0) // S5_GROUP
    e = (T - 1) - rows
    one = jnp.where(_lo_mask((T * S5_GROUP, 2 * P)), 1.0, 0.0).astype(F32)
    apow = one
    for b in range(4):
        apow = jnp.where(((e >> b) & 1) == 1, _cmul(apow, pw[b]), apow)
    af = _cmul(apow, f)
    w_ref[0] = _cmul(af, btt_ref[0])

    col = [jnp.transpose(jnp.broadcast_to(p, (2 * P, 2 * P))) for p in pw[:4]]
    lane_e = lax.broadcasted_iota(jnp.int32, (P, T * S5_GROUP), 1) // S5_GROUP

    def col_cmul(xr, xi, c):
        cr = jnp.concatenate([c[:P], c[:P]], axis=1)
        ci = jnp.concatenate([c[P:], c[P:]], axis=1)
        return xr * cr - xi * ci, xr * ci + xi * cr

    pr = jnp.ones((P, T * S5_GROUP), F32)
    pi = jnp.zeros((P, T * S5_GROUP), F32)
    for b in range(4):
        nr_, ni_ = col_cmul(pr, pi, col[b])
        sel = ((lane_e >> b) & 1) == 1
        pr = jnp.where(sel, nr_, pr)
        pi = jnp.where(sel, ni_, pi)
    ctr = ct_ref[0, 0]
    cti = ct_ref[0, 1]
    qr = pr * ctr - pi * cti
    qi = pr * cti + pi * ctr
    q1r, q1i = col_cmul(qr, qi, col[0])
    v_ref[0] = jnp.concatenate([q1r, -q1i], axis=0).astype(v_ref.dtype)

    fr = jnp.where(lo, f, _swap(f))[:, :P]
    fi = jnp.where(lo, _swap(f), f)[:, :P]
    btr = bt_ref[0, 0]
    bti = bt_ref[0, 1]
    bfr = btr * fr - bti * fi
    bfi = btr * fi + bti * fr
    hp = lax.Precision.HIGHEST
    kk = (jnp.dot(bfr, qr, precision=hp, preferred_element_type=F32)
          - jnp.dot(bfi, qi, precision=hp, preferred_element_type=F32))
    lane = lax.broadcasted_iota(jnp.int32, kk.shape, 1)
    for s in range(T):
        blk = kk if s == 0 else jnp.where(lane >= S5_GROUP * s, pltpu.roll(kk, S5_GROUP * s, axis=1), 0.0)
        m_ref[0, S5_GROUP * s:S5_GROUP * (s + 1), :] = blk.astype(m_ref.dtype)


def _s5_prep(a_re, a_im, log_dt, b_re, b_im, c_re, c_im):
    G, P, T = S5_GROUPS, S5_STATE, S5_T
    a_pk = jnp.concatenate([a_re, a_im], axis=-1).reshape(G, 1, 2 * P)
    dt_pk = jnp.broadcast_to(log_dt.reshape(G, 1, 1), (G, 1, 2 * P))
    bt = jnp.stack([b_re, b_im], axis=1).transpose(0, 1, 3, 2)
    btt = jnp.tile(jnp.concatenate([b_re, b_im], axis=1).transpose(0, 2, 1), (1, T, 1))
    ct = jnp.tile(jnp.stack([c_re, c_im], axis=1).transpose(0, 1, 3, 2), (1, 1, 1, T))
    n = T * S5_GROUP
    return pl.pallas_call(
        _s5_prep_kernel,
        grid=(G,),
        in_specs=[
            pl.BlockSpec((1, 1, 2 * P), lambda g: (g, 0, 0)),
            pl.BlockSpec((1, 1, 2 * P), lambda g: (g, 0, 0)),
            pl.BlockSpec((1, 2, S5_GROUP, P), lambda g: (g, 0, 0, 0)),
            pl.BlockSpec((1, n, 2 * P), lambda g: (g, 0, 0)),
            pl.BlockSpec((1, 2, P, n), lambda g: (g, 0, 0, 0)),
        ],
        out_specs=[
            pl.BlockSpec((1, n, n), lambda g: (g, 0, 0)),
            pl.BlockSpec((1, n, 2 * P), lambda g: (g, 0, 0)),
            pl.BlockSpec((1, 2 * P, n), lambda g: (g, 0, 0)),
            pl.BlockSpec((1, 8, 2 * P), lambda g: (g, 0, 0)),
        ],
        out_shape=[
            jax.ShapeDtypeStruct((G, n, n), BF16),
            jax.ShapeDtypeStruct((G, n, 2 * P), F32),
            jax.ShapeDtypeStruct((G, 2 * P, n), BF16),
            jax.ShapeDtypeStruct((G, 8, 2 * P), F32),
        ],
        compiler_params=_cparams("parallel"),
        name="s5_prep",
    )(a_pk, dt_pk, bt, btt, ct)


def _s5_scan_kernel(u_ref, m_ref, w_ref, v_ref, z_ref, y_ref, hf_ref):
    u = u_ref[0]
    C = u.shape[0]
    s = jnp.dot(u, w_ref[0].astype(BF16), preferred_element_type=F32)
    z = z_ref[0, 4:5, :]
    row = lax.broadcasted_iota(jnp.int32, s.shape, 0)
    h = s
    d = 1
    while d < C:
        hs = jnp.where(row >= d, pltpu.roll(h, d, axis=0), 0.0)
        h = h + _cmul(hs, z)
        z = _cmul(z, z)
        d *= 2
    hprev = jnp.where(row >= 1, pltpu.roll(h, 1, axis=0), 0.0)
    y = (jnp.dot(u, m_ref[0], preferred_element_type=F32)
         + jnp.dot(hprev.astype(BF16), v_ref[0], preferred_element_type=F32))
    y_ref[0] = y.astype(y_ref.dtype)
    hf_ref[0] = h[C - 8:, :]


def _s5_step_kernel(u_ref, h0_ref, m_ref, w_ref, v_ref, z_ref, y_ref, hf_ref, *, t_len):
    n = t_len * S5_GROUP
    off = (S5_T - t_len) * S5_GROUP
    u = u_ref[0]
    h0 = h0_ref[0]
    hp = lax.Precision.HIGHEST
    s = jnp.dot(u, w_ref[0, off:, :], precision=hp, preferred_element_type=F32)
    z = z_ref[0, 0:1, :] if t_len * 2 == S5_T else z_ref[0, 4:5, :]
    hf_ref[0] = _cmul(h0, z) + s
    y = (jnp.dot(u.astype(BF16), m_ref[0, :n, :n], preferred_element_type=F32)
         + jnp.dot(h0.astype(BF16), v_ref[0, :, :n], preferred_element_type=F32))
    y_ref[0] = y.astype(y_ref.dtype)


def _s5_core_prompt(x, mats):
    m, w, v, z = mats
    L = x.shape[0]
    G, T = S5_GROUPS, S5_T
    C = L // T
    n = T * S5_GROUP
    u = x.reshape(C, T, G, S5_GROUP).transpose(2, 0, 1, 3).reshape(G, C, n).astype(BF16)
    y, hf = pl.pallas_call(
        _s5_scan_kernel,
        grid=(G,),
        in_specs=[
            pl.BlockSpec((1, C, n), lambda g: (g, 0, 0)),
            pl.BlockSpec((1, n, n), lambda g: (g, 0, 0)),
            pl.BlockSpec((1, n, 2 * S5_STATE), lambda g: (g, 0, 0)),
            pl.BlockSpec((1, 2 * S5_STATE, n), lambda g: (g, 0, 0)),
            pl.BlockSpec((1, 8, 2 * S5_STATE), lambda g: (g, 0, 0)),
        ],
        out_specs=[
            pl.BlockSpec((1, C, n), lambda g: (g, 0, 0)),
            pl.BlockSpec((1, 8, 2 * S5_STATE), lambda g: (g, 0, 0)),
        ],
        out_shape=[
            jax.ShapeDtypeStruct((G, C, n), BF16),
            jax.ShapeDtypeStruct((G, 8, 2 * S5_STATE), F32),
        ],
        compiler_params=_cparams("parallel"),
        name="s5_scan",
    )(u, m, w, v, z)
    y = y.reshape(G, C, T, S5_GROUP).transpose(1, 2, 0, 3).reshape(L, D_MODEL)
    return y, hf[:, 7, :]


def _s5_core_sample(x, h_re, h_im, mats):
    m, w, v, z = mats
    B, t_len, _ = x.shape
    assert t_len in (S5_T // 2, S5_T)
    G = S5_GROUPS
    n = t_len * S5_GROUP
    u = x.reshape(B, t_len, G, S5_GROUP).transpose(2, 0, 1, 3).reshape(G, B, n)
    h0 = jnp.concatenate([h_re, h_im], axis=-1).transpose(1, 0, 2)
    nf = S5_T * S5_GROUP
    y, hf = pl.pallas_call(
        functools.partial(_s5_step_kernel, t_len=t_len),
        grid=(G,),
        in_specs=[
            pl.BlockSpec((1, B, n), lambda g: (g, 0, 0)),
            pl.BlockSpec((1, B, 2 * S5_STATE), lambda g: (g, 0, 0)),
            pl.BlockSpec((1, nf, nf), lambda g: (g, 0, 0)),
            pl.BlockSpec((1, nf, 2 * S5_STATE), lambda g: (g, 0, 0)),
            pl.BlockSpec((1, 2 * S5_STATE, nf), lambda g: (g, 0, 0)),
            pl.BlockSpec((1, 8, 2 * S5_STATE), lambda g: (g, 0, 0)),
        ],
        out_specs=[
            pl.BlockSpec((1, B, n), lambda g: (g, 0, 0)),
            pl.BlockSpec((1, B, 2 * S5_STATE), lambda g: (g, 0, 0)),
        ],
        out_shape=[
            jax.ShapeDtypeStruct((G, B, n), BF16),
            jax.ShapeDtypeStruct((G, B, 2 * S5_STATE), F32),
        ],
        compiler_params=_cparams("parallel"),
        name="s5_step",
    )(u, h0, m, w, v, z)
    y = y.reshape(G, B, t_len, S5_GROUP).transpose(1, 2, 0, 3).reshape(B * t_len, D_MODEL)
    return y, hf


def _s5_glu_ln_kernel(x_ref, y_ref, d_ref, w_ref, g_ref, b_ref, o_ref):
    x = x_ref[...]
    yy = y_ref[...].astype(F32) + x * d_ref[...]
    zg = jax.nn.gelu(yy).astype(BF16)
    zz = jnp.dot(zg, w_ref[...], preferred_element_type=F32)
    out = zz[:, :D_MODEL] * jax.nn.sigmoid(zz[:, D_MODEL:])
    o_ref[...] = _layer_norm(ALPHA * x + out, g_ref[...], b_ref[...])


def _row_tile(n, cap):
    t = min(n, cap)
    while n % t:
        t //= 2
    return t


def _s5_glu_ln(x, y, d_skip, w_glu, g, b):
    n = x.shape[0]
    tm = _row_tile(n, 512)
    const = lambda i: (0, 0)
    return pl.pallas_call(
        _s5_glu_ln_kernel,
        grid=(n // tm,),
        in_specs=[
            pl.BlockSpec((tm, D_MODEL), lambda i: (i, 0)),
            pl.BlockSpec((tm, D_MODEL), lambda i: (i, 0)),
            pl.BlockSpec((1, D_MODEL), const),
            pl.BlockSpec((D_MODEL, 2 * D_MODEL), const),
            pl.BlockSpec((1, D_MODEL), const),
            pl.BlockSpec((1, D_MODEL), const),
        ],
        out_specs=pl.BlockSpec((tm, D_MODEL), lambda i: (i, 0)),
        out_shape=jax.ShapeDtypeStruct((n, D_MODEL), F32),
        compiler_params=_cparams("parallel"),
        name="s5_glu_ln",
    )(x, y, d_skip.reshape(1, -1), w_glu.astype(BF16), g.reshape(1, -1), b.reshape(1, -1))


def _ffn_kernel(x_ref, p1_ref, p2_ref, wu_ref, cw_ref, cb_ref, wd_ref, g_ref, b_ref,
                o_ref, st_ref, carry_ref, *, period, chained):
    i = pl.program_id(0)
    x = x_ref[...]
    xb = x.astype(BF16)
    tm = x.shape[0]
    row = lax.broadcasted_iota(jnp.int32, (tm, 1), 0)
    t = row % period
    if chained:
        @pl.when(i == 0)
        def _():
            carry_ref[...] = jnp.zeros_like(carry_ref)

    def conv(cols):
        h = jnp.dot(xb, wu_ref[:, cols], preferred_element_type=F32)
        h1 = jnp.where(t >= 1, pltpu.roll(h, 1, axis=0), 0.0)
        h2 = jnp.where(t >= 2, pltpu.roll(h, 2, axis=0), 0.0)
        if chained:
            c0 = carry_ref[6:7, cols]
            c1 = carry_ref[7:8, cols]
            h1 = h1 + jnp.where(row == 0, c1, 0.0)
            h2 = h2 + jnp.where(row == 0, c0, 0.0) + jnp.where(row == 1, c1, 0.0)
            carry_ref[:, cols] = h[tm - 8:, :]
            st_ref[:, cols] = h[tm - 8:, :]
        else:
            h1 = h1 + p1_ref[:, cols]
            h2 = h2 + p2_ref[:, cols]
            st_ref[:, cols] = h
        return cb_ref[:, cols] + cw_ref[0:1, cols] * h2 + cw_ref[1:2, cols] * h1 + cw_ref[2:3, cols] * h

    f = jnp.zeros((tm, D_MODEL), F32)
    for c in range(D_FF // FFN_CHUNK):
        ca = slice(c * FFN_CHUNK, (c + 1) * FFN_CHUNK)
        cv = slice(D_FF + c * FFN_CHUNK, D_FF + (c + 1) * FFN_CHUNK)
        act = jax.nn.silu(conv(ca)) * conv(cv)
        f = f + jnp.dot(act.astype(BF16), wd_ref[ca, :], preferred_element_type=F32)
    o_ref[...] = _layer_norm(ALPHA * x + f, g_ref[...], b_ref[...])


def _ffn(x, buf, w_up, conv_w, conv_b, w_down, g, b, *, seq_len):
    n = x.shape[0]
    n_seq = n // seq_len
    f2 = 2 * D_FF
    chained = buf is None
    if chained:
        assert n_seq == 1
        tm = _row_tile(n, 512)
        p1 = p2 = jnp.zeros((8, f2), F32)
        pspec = pl.BlockSpec((8, f2), lambda i: (0, 0))
        st_shape, st_spec = (8, f2), pl.BlockSpec((8, f2), lambda i: (0, 0))
    else:
        tm = n
        z = jnp.zeros((n_seq, seq_len, f2), F32)
        p1 = z.at[:, 0].set(buf[:, 1]).reshape(n, f2)
        p2 = z.at[:, 0].set(buf[:, 0]).at[:, 1].set(buf[:, 1]).reshape(n, f2)
        pspec = pl.BlockSpec((tm, f2), lambda i: (i, 0))
        st_shape, st_spec = (n, f2), pl.BlockSpec((tm, f2), lambda i: (i, 0))
    const = lambda i: (0, 0)
    out, st = pl.pallas_call(
        functools.partial(_ffn_kernel, period=seq_len, chained=chained),
        grid=(n // tm,),
        in_specs=[
            pl.BlockSpec((tm, D_MODEL), lambda i: (i, 0)),
            pspec, pspec,
            pl.BlockSpec((D_MODEL, f2), const, pipeline_mode=pl.Buffered(1)),
            pl.BlockSpec((CONV_W, f2), const),
            pl.BlockSpec((1, f2), const),
            pl.BlockSpec((D_FF, D_MODEL), const, pipeline_mode=pl.Buffered(1)),
            pl.BlockSpec((1, D_MODEL), const),
            pl.BlockSpec((1, D_MODEL), const),
        ],
        out_specs=[pl.BlockSpec((tm, D_MODEL), lambda i: (i, 0)), st_spec],
        out_shape=[jax.ShapeDtypeStruct((n, D_MODEL), F32), jax.ShapeDtypeStruct(st_shape, F32)],
        scratch_shapes=[pltpu.VMEM((8, f2), F32)],
        compiler_params=_cparams("arbitrary"),
        name="conv_ffn",
    )(x, p1, p2, w_up.astype(BF16), conv_w, conv_b.reshape(1, -1), w_down.astype(BF16),
      g.reshape(1, -1), b.reshape(1, -1))
    if chained:
        state = st[6:8][None]
    else:
        state = st.reshape(n_seq, seq_len, f2)[:, seq_len - 2:]
    return out, state


Q_SCALE = HEAD_DIM ** -0.5 * math.log2(math.e)
TQ = 128
TK = 512


def _perm_heads():
    idx = jnp.arange(D_MODEL).reshape(N_KV, HEADS_PER_KV, HEAD_DIM)
    return idx.transpose(1, 0, 2).reshape(-1)


def _nsa_proj_kernel(x_ref, wq_ref, wkv_ref, wg_ref, *refs, rows_t):
    xb = x_ref[...].astype(BF16)
    q = jnp.dot(xb, wq_ref[...], preferred_element_type=F32) * Q_SCALE
    kv = jnp.dot(xb, wkv_ref[...], preferred_element_type=F32)
    gates = jax.nn.sigmoid(jnp.dot(xb, wg_ref[...], preferred_element_type=F32))
    if rows_t:
        wkvt_ref, q_ref, kc_ref, vc_ref, kvb_ref, g_ref = refs[:6]
        kvt = lax.dot_general(wkvt_ref[...], xb, (((1,), (1,)), ((), ())), preferred_element_type=F32)
        for j, r in enumerate(refs[6:]):
            r[...] = kvt[j * KV_DIM:(j + 1) * KV_DIM, :]
    else:
        q_ref, kc_ref, vc_ref, kvb_ref, g_ref = refs[:5]
        for j, r in enumerate(refs[5:]):
            r[...] = kv[:, (j + 2) * KV_DIM:(j + 3) * KV_DIM]
    q_ref[...] = q.astype(BF16)
    kc_ref[...] = kv[:, :KV_DIM]
    vc_ref[...] = kv[:, KV_DIM:2 * KV_DIM]
    kvb_ref[...] = kv[:, 2 * KV_DIM:].astype(BF16)
    g_ref[...] = gates.astype(BF16)


def _gate_columns():
    br = jnp.arange(3)[:, None, None, None]
    h = jnp.arange(HEADS_PER_KV)[None, :, None, None]
    g = jnp.arange(N_KV)[None, None, :, None]
    idx = (g * HEADS_PER_KV + h) * 3 + br + jnp.zeros((1, 1, 1, HEAD_DIM), jnp.int32)
    return idx.reshape(-1)


def _nsa_proj(x, w_in, *, rows_t):
    n = x.shape[0]
    tm = _row_tile(n, 512)
    hd = N_HEADS * HEAD_DIM
    wq = w_in[:, :hd][:, _perm_heads()].astype(BF16)
    wkv = w_in[:, hd:hd + 6 * KV_DIM].astype(BF16)
    wg = w_in[:, hd + 6 * KV_DIM:][:, _gate_columns()].astype(BF16)
    const = lambda i: (0, 0)
    rows = lambda w: pl.BlockSpec((tm, w), lambda i: (i, 0))
    in_specs = [rows(D_MODEL), pl.BlockSpec((D_MODEL, hd), const), pl.BlockSpec((D_MODEL, 6 * KV_DIM), const),
                pl.BlockSpec((D_MODEL, 3 * hd), const)]
    out_specs = [rows(hd), rows(KV_DIM), rows(KV_DIM), rows(4 * KV_DIM), rows(3 * hd)]
    out_shape = [jax.ShapeDtypeStruct((n, hd), BF16), jax.ShapeDtypeStruct((n, KV_DIM), F32),
                 jax.ShapeDtypeStruct((n, KV_DIM), F32), jax.ShapeDtypeStruct((n, 4 * KV_DIM), BF16),
                 jax.ShapeDtypeStruct((n, 3 * hd), BF16)]
    args = [x, wq, wkv, wg]
    if rows_t:
        in_specs.append(pl.BlockSpec((6 * KV_DIM, D_MODEL), const))
        args.append(wkv.T)
        out_specs += [pl.BlockSpec((KV_DIM, tm), lambda i: (0, i))] * 6
        out_shape += [jax.ShapeDtypeStruct((KV_DIM, n), F32)] * 6
    else:
        out_specs += [rows(KV_DIM)] * 4
        out_shape += [jax.ShapeDtypeStruct((n, KV_DIM), F32)] * 4
    return pl.pallas_call(
        functools.partial(_nsa_proj_kernel, rows_t=rows_t),
        grid=(n // tm,),
        in_specs=in_specs,
        out_specs=out_specs,
        out_shape=out_shape,
        compiler_params=_cparams("parallel"),
        name="nsa_proj",
    )(*args)


def _cmp_weights(pe, w1, b1, w2):
    eye = jnp.eye(N_KV, dtype=F32)
    w1b = jnp.einsum('jsdh,ge->jsgdeh', w1.reshape(2, CMP_STRIDE, HEAD_DIM, CMP_HIDDEN), eye)
    w1b = w1b.reshape(2, CMP_STRIDE * KV_DIM, N_KV * CMP_HIDDEN).astype(BF16)
    w2b = jnp.einsum('hd,ge->ghed', w2, eye).reshape(N_KV * CMP_HIDDEN, KV_DIM).astype(BF16)
    peb = jnp.broadcast_to(pe.reshape(2, CMP_STRIDE, 1, HEAD_DIM), (2, CMP_STRIDE, N_KV, HEAD_DIM))
    peb = peb.reshape(2, 1, CMP_STRIDE * KV_DIM)
    b1b = jnp.tile(b1, N_KV).reshape(1, N_KV * CMP_HIDDEN)
    return peb, w1b, b1b, w2b


def _compress_rows(x, carry_ref, pe_ref, w1_ref, b1_ref, w2_ref):
    p0 = jnp.dot((x + pe_ref[0]).astype(BF16), w1_ref[0], preferred_element_type=F32)
    p1 = jnp.dot((x + pe_ref[1]).astype(BF16), w1_ref[1], preferred_element_type=F32)
    row = lax.broadcasted_iota(jnp.int32, (x.shape[0], 1), 0)
    p0s = jnp.where(row == 0, carry_ref[7:8, :], pltpu.roll(p0, 1, axis=0))
    carry_ref[...] = p0[x.shape[0] - 8:, :]
    h = b1_ref[...] + p0s + p1
    return jnp.dot(jax.nn.gelu(h).astype(BF16), w2_ref[...], preferred_element_type=F32)


def _compress_prompt_kernel(x_ref, pe_ref, w1_ref, b1_ref, w2_ref, o_ref, carry_ref):
    @pl.when(pl.program_id(0) == 0)
    def _():
        carry_ref[...] = jnp.zeros_like(carry_ref)
    o_ref[...] = _compress_rows(x_ref[...], carry_ref, pe_ref, w1_ref, b1_ref, w2_ref).astype(o_ref.dtype)


def _compress_prompt(rows, cw):
    peb, w1b, b1b, w2b = cw
    n = rows.shape[0]
    nc = n // CMP_STRIDE
    wide = CMP_STRIDE * KV_DIM
    x = rows.reshape(nc, wide)
    tc = _row_tile(nc, 256)
    return pl.pallas_call(
        _compress_prompt_kernel,
        grid=(nc // tc,),
        in_specs=[pl.BlockSpec((tc, wide), lambda i: (i, 0)),
                  pl.BlockSpec((2, 1, wide), lambda i: (0, 0, 0)),
                  pl.BlockSpec((2, wide, N_KV * CMP_HIDDEN), lambda i: (0, 0, 0)),
                  pl.BlockSpec((1, N_KV * CMP_HIDDEN), lambda i: (0, 0)),
                  pl.BlockSpec((N_KV * CMP_HIDDEN, KV_DIM), lambda i: (0, 0))],
        out_specs=pl.BlockSpec((tc, KV_DIM), lambda i: (i, 0)),
        out_shape=jax.ShapeDtypeStruct((nc, KV_DIM), BF16),
        scratch_shapes=[pltpu.VMEM((8, N_KV * CMP_HIDDEN), F32)],
        compiler_params=_cparams("arbitrary"),
        name="nsa_compress",
    )(x, peb, w1b, b1b, w2b)


def _masked_softmax_rows(s, valid):
    s = jnp.where(valid, s, NEG_BIG)
    m = jnp.max(s, -1, keepdims=True)
    e = jnp.where(valid, jnp.exp2(s - m), 0.0)
    den = jnp.sum(e, -1, keepdims=True)
    return e / jnp.where(den > 0, den, 1.0)


def _biased_softmax_rows(s, bias, row_ok):
    s = s + bias
    m = jnp.max(s, -1, keepdims=True)
    e = jnp.exp2(s - m)
    den = jnp.sum(e, -1, keepdims=True)
    return e * jnp.where(row_ok, 1.0 / den, 0.0)


def _group_lanes(per_group):
    lo = lax.broadcasted_iota(jnp.int32, (1, 128), 1) < HEAD_DIM
    return jnp.concatenate([jnp.where(lo, per_group[2 * c], per_group[2 * c + 1]) for c in range(N_KV // 2)], axis=1)


def _select_blocks(imp, cur, n_sel, axis=1):
    ns = imp.shape[axis]
    jj = lax.broadcasted_iota(jnp.int32, (1, ns) if axis == 1 else (ns, 1), axis)
    forced = (jj == 0) | ((jj <= cur) & (jj >= cur - 1))
    v = jnp.where(jj > cur, -FORCE, jnp.where(forced, FORCE, imp))
    sel = jnp.zeros(imp.shape, jnp.bool_)
    for _ in range(min(n_sel, ns)):
        m = jnp.max(v, axis, keepdims=True)
        idx = jnp.min(jnp.where(v == m, jj, ns), axis, keepdims=True)
        hit = jj == idx
        sel = sel | hit
        v = jnp.where(hit, -jnp.inf, v)
    return sel


def _nsa_attn_prompt_kernel(q_ref, gate_ref, ks_ref, vs_ref, *rest):
    nwb = WINDOW // TQ + 1
    kw_refs, vw_refs = rest[:nwb], rest[nwb:2 * nwb]
    kc_ref, vc_ref, c2st_ref, o_ref, qg_ref, selb_ref, m_ref, l_ref, acc_ref, oo_ref = rest[2 * nwb:]
    i = pl.program_id(0)
    nh = HEADS_PER_KV
    nc = kc_ref.shape[0]
    ns = c2st_ref.shape[0]
    rows = nh * TQ
    hd = N_HEADS * HEAD_DIM
    q = q_ref[...]
    qh = jnp.concatenate([q[:, KV_DIM * h:KV_DIM * (h + 1)] for h in range(nh)], axis=0)
    lane_g = lax.broadcasted_iota(jnp.int32, (1, KV_DIM), 1) // HEAD_DIM
    qpos_r = i * TQ + lax.broadcasted_iota(jnp.int32, (rows, 1), 0) % TQ
    qpos_q = i * TQ + lax.broadcasted_iota(jnp.int32, (TQ, 1), 0)
    cur_t = (i * TQ + lax.broadcasted_iota(jnp.int32, (1, TQ), 1)) // SEL_BLOCK
    nt = (((i + 1) * TQ + TK - 1) // TK)
    nt_dims = (((1,), (1,)), ((), ()))

    def gate_rows(br):
        return jnp.concatenate([gate_ref[:, br * hd + KV_DIM * h:br * hd + KV_DIM * (h + 1)] for h in range(nh)],
                               axis=0).astype(F32)

    kwin = jnp.concatenate([r[...] for r in kw_refs], axis=0)
    vwin = jnp.concatenate([r[...] for r in vw_refs], axis=0)
    kw_pos = i * TQ - WINDOW + lax.broadcasted_iota(jnp.int32, (1, WINDOW + TQ), 1)
    dlt = qpos_r - kw_pos
    bias_w = jnp.where((dlt >= 0) & (dlt < WINDOW) & (kw_pos >= 0), 0.0, NEG_BIG)
    n_idx = lax.broadcasted_iota(jnp.int32, (1, nc), 1)
    bias_c = jnp.where((n_idx >= 1) & (n_idx * CMP_STRIDE + (CMP_STRIDE - 1) <= qpos_r), 0.0, NEG_BIG)
    row_ok_c = qpos_r >= 2 * CMP_STRIDE - 1
    g_c, g_w = gate_rows(0), gate_rows(2)

    impt = []
    for g in range(N_KV):
        qg = jnp.where(lane_g == g, qh, jnp.zeros_like(qh))
        qg_ref[g] = qg
        s_c = lax.dot_general(qg, kc_ref[...], nt_dims, preferred_element_type=F32)
        p_c = _biased_softmax_rows(s_c, bias_c, row_ok_c)
        o_c = jnp.dot(p_c.astype(BF16), vc_ref[...], preferred_element_type=F32)
        pcs = p_c[0:TQ]
        for h in range(1, nh):
            pcs = pcs + p_c[h * TQ:(h + 1) * TQ]
        acc_i = jnp.zeros((ns, TQ), F32)
        rem = pcs
        for _ in range(3):
            part = rem.astype(BF16)
            rem = rem - part.astype(F32)
            acc_i = acc_i + lax.dot_general(c2st_ref[...], part, nt_dims, preferred_element_type=F32)
        impt.append(acc_i)
        s_w = lax.dot_general(qg, kwin, nt_dims, preferred_element_type=F32)
        p_w = _biased_softmax_rows(s_w, bias_w, True)
        o_w = jnp.dot(p_w.astype(BF16), vwin, preferred_element_type=F32)
        oo = jnp.where(lane_g == g, g_c * o_c + g_w * o_w, oo) if g else g_c * o_c + g_w * o_w
        m_ref[g] = jnp.full((rows, 128), NEG_BIG, F32)
        l_ref[g] = jnp.zeros((rows, 128), F32)
    oo_ref[...] = oo
    acc_ref[...] = jnp.zeros((rows, KV_DIM), F32)
    sel_t = _select_blocks(jnp.concatenate(impt, axis=1), jnp.concatenate([cur_t] * N_KV, axis=1), N_SEL, axis=0)
    sel_t = sel_t.astype(F32)
    for g in range(N_KV):
        selb_ref[g] = jnp.transpose(sel_t[:, g * TQ:(g + 1) * TQ]).astype(BF16)

    j_row = lax.broadcasted_iota(jnp.int32, (ns, TK), 0)
    j_of_lane = lax.broadcasted_iota(jnp.int32, (ns, TK), 1) // SEL_BLOCK
    key_lane = lax.broadcasted_iota(jnp.int32, (1, TK), 1)

    def body(kt, carry):
        k0 = pl.multiple_of(kt * TK, TK)
        kblk = ks_ref[pl.ds(k0, TK), :]
        vblk = vs_ref[pl.ds(k0, TK), :]
        onehot = (j_row == kt * (TK // SEL_BLOCK) + j_of_lane).astype(BF16)
        causal = k0 + key_lane <= qpos_q
        alphas = []
        pv = None
        for g in range(N_KV):
            picked = jnp.dot(selb_ref[g], onehot, preferred_element_type=F32)
            bias = jnp.where((picked > 0.5) & causal, 0.0, NEG_BIG)
            s = lax.dot_general(qg_ref[g], kblk, nt_dims, preferred_element_type=F32)
            s = s + jnp.concatenate([bias] * nh, axis=0)
            m_prev = m_ref[g]
            m_new = jnp.maximum(m_prev, jnp.max(s, -1, keepdims=True))
            alpha = jnp.exp2(m_prev - m_new)
            p = jnp.exp2(s - jnp.concatenate([m_new] * (TK // 128), axis=1))
            l_ref[g] = alpha * l_ref[g] + jnp.sum(p, -1, keepdims=True)
            m_ref[g] = m_new
            alphas.append(alpha)
            pv_g = jnp.dot(p.astype(BF16), jnp.where(lane_g == g, vblk, jnp.zeros_like(vblk)),
                           preferred_element_type=F32)
            pv = pv_g if pv is None else pv + pv_g
        acc_ref[...] = _group_lanes(alphas) * acc_ref[...] + pv
        return carry

    lax.fori_loop(0, (nt + 1) // 2, lambda kk, c: body(2 * kk + 1, body(2 * kk, c)), 0)

    o_s = acc_ref[...] / _group_lanes([l_ref[g] for g in range(N_KV)])
    total = oo_ref[...] + gate_rows(1) * o_s
    for h in range(nh):
        o_ref[:, KV_DIM * h:KV_DIM * (h + 1)] = total[h * TQ:(h + 1) * TQ].astype(o_ref.dtype)


def _cmp_to_sel(nc, ns):
    blk = jnp.arange(nc)[:, None] - 1
    i = blk * CMP_STRIDE
    j = jnp.arange(ns)[None, :] * SEL_BLOCK
    return ((blk >= 0) & (i < j + SEL_BLOCK) & (i + CMP_LEN > j)).astype(F32)


def _nsa_attn_prompt(q, gates, kvb, kc, vc):
    n = q.shape[0]
    nc, ns = n // CMP_STRIDE, n // SEL_BLOCK
    assert n % TK == 0 and WINDOW % TQ == 0
    wpad = jnp.pad(kvb[:, 2 * KV_DIM:], ((WINDOW, 0), (0, 0)))
    c2st = _cmp_to_sel(nc, ns).T.astype(BF16)
    nwb = WINDOW // TQ + 1
    rows = HEADS_PER_KV * TQ
    const2 = lambda i: (0, 0)
    one = pl.Buffered(1)
    in_specs = [
        pl.BlockSpec((TQ, D_MODEL), lambda i: (i, 0)),
        pl.BlockSpec((TQ, 3 * D_MODEL), lambda i: (i, 0)),
        pl.BlockSpec((n, KV_DIM), lambda i: (0, 0), pipeline_mode=one),
        pl.BlockSpec((n, KV_DIM), lambda i: (0, 1), pipeline_mode=one),
    ]
    in_specs += [pl.BlockSpec((TQ, KV_DIM), functools.partial(lambda i, k: (i + k, 0), k=k)) for k in range(nwb)]
    in_specs += [pl.BlockSpec((TQ, KV_DIM), functools.partial(lambda i, k: (i + k, 1), k=k)) for k in range(nwb)]
    in_specs += [
        pl.BlockSpec((nc, KV_DIM), const2, pipeline_mode=one),
        pl.BlockSpec((nc, KV_DIM), const2, pipeline_mode=one),
        pl.BlockSpec((ns, nc), const2, pipeline_mode=one),
    ]
    return pl.pallas_call(
        _nsa_attn_prompt_kernel,
        grid=(n // TQ,),
        in_specs=in_specs,
        out_specs=pl.BlockSpec((TQ, D_MODEL), lambda i: (i, 0)),
        out_shape=jax.ShapeDtypeStruct((n, D_MODEL), BF16),
        scratch_shapes=[pltpu.VMEM((N_KV, rows, KV_DIM), BF16), pltpu.VMEM((N_KV, TQ, ns), BF16),
                        pltpu.VMEM((N_KV, rows, 128), F32), pltpu.VMEM((N_KV, rows, 128), F32),
                        pltpu.VMEM((rows, KV_DIM), F32), pltpu.VMEM((rows, KV_DIM), F32)],
        compiler_params=_cparams("parallel"),
        name="nsa_attn_prompt",
    )(q, gates, kvb, kvb, *([wpad] * (2 * nwb)), kc, vc, c2st)


def _out_proj_ln_kernel(x_ref, o_ref, w_ref, g_ref, b_ref, y_ref):
    y = jnp.dot(o_ref[...], w_ref[...], preferred_element_type=F32)
    y_ref[...] = _layer_norm(ALPHA * x_ref[...] + y, g_ref[...], b_ref[...])


def _out_proj_ln(x, o, w_o, g, b):
    n = x.shape[0]
    tm = _row_tile(n, 512)
    const = lambda i: (0, 0)
    return pl.pallas_call(
        _out_proj_ln_kernel,
        grid=(n // tm,),
        in_specs=[pl.BlockSpec((tm, D_MODEL), lambda i: (i, 0)), pl.BlockSpec((tm, D_MODEL), lambda i: (i, 0)),
                  pl.BlockSpec((D_MODEL, D_MODEL), const), pl.BlockSpec((1, D_MODEL), const),
                  pl.BlockSpec((1, D_MODEL), const)],
        out_specs=pl.BlockSpec((tm, D_MODEL), lambda i: (i, 0)),
        out_shape=jax.ShapeDtypeStruct((n, D_MODEL), F32),
        compiler_params=_cparams("parallel"),
        name="nsa_out_ln",
    )(x, o, w_o[_perm_heads(), :].astype(BF16), g.reshape(1, -1), b.reshape(1, -1))


def _nsa_prompt_layer(x, w_in, pe, w1, b1, w2, w_o, g, b):
    q, k_c, v_c, kvb, gates, *rows_t = _nsa_proj(x, w_in, rows_t=True)
    kc = _compress_prompt(k_c, _cmp_weights(pe[0], w1[0], b1[0], w2[0]))
    vc = _compress_prompt(v_c, _cmp_weights(pe[1], w1[1], b1[1], w2[1]))
    o = _nsa_attn_prompt(q, gates, kvb, kc, vc)
    y = _out_proj_ln(x, o, w_o, g, b)
    return y, tuple(rows_t)


def _compress_sample_kernel(pt_ref, *refs, n_pg):
    pages = refs[:n_pg]
    new_ref, pe_ref, w1_ref, b1_ref, w2_ref, o_ref, xa_ref, xb_ref, carry_ref = refs[n_pg:]
    s = pl.program_id(1)
    new_tile = pl.num_programs(1) - 2
    tc = n_pg * (PAGE_SIZE // CMP_STRIDE)
    nl = KV_DIM // 128
    nh = N_KV * CMP_HIDDEN

    @pl.when(s == 0)
    def _():
        xa_ref[...] = jnp.zeros_like(xa_ref)
        xb_ref[...] = jnp.zeros_like(xb_ref)

    @pl.when(s == 1)
    def _():
        carry_ref[...] = jnp.zeros_like(carry_ref)

    def fill(dst_ref):
        for k, p in enumerate(pages):
            xt = jnp.transpose(p[0])
            if k == 0:
                head = jnp.where(s == new_tile, new_ref[0], xt[:CMP_STRIDE])
                xt = jnp.concatenate([head, xt[CMP_STRIDE:]], axis=0)
            for c in range(nl):
                dst_ref[c, k * PAGE_SIZE:(k + 1) * PAGE_SIZE, :] = xt[:, c * 128:(c + 1) * 128]

    def compress(src_ref):
        p0 = jnp.zeros((tc, nh), F32)
        p1 = jnp.zeros((tc, nh), F32)
        for ss in range(CMP_STRIDE):
            x = jnp.concatenate([src_ref[c, pl.ds(ss, tc, stride=CMP_STRIDE), :] for c in range(nl)], axis=1)
            p0 = p0 + jnp.dot((x + pe_ref[0, ss:ss + 1, :]).astype(BF16), w1_ref[0, ss], preferred_element_type=F32)
            p1 = p1 + jnp.dot((x + pe_ref[1, ss:ss + 1, :]).astype(BF16), w1_ref[1, ss], preferred_element_type=F32)
        row = lax.broadcasted_iota(jnp.int32, (tc, 1), 0)
        p0s = jnp.where(row == 0, carry_ref[7:8, :], pltpu.roll(p0, 1, axis=0))
        carry_ref[...] = p0[tc - 8:, :]
        h = b1_ref[...] + p0s + p1
        o_ref[0] = jnp.dot(jax.nn.gelu(h).astype(BF16), w2_ref[...], preferred_element_type=F32).astype(o_ref.dtype)

    @pl.when(s % 2 == 0)
    def _():
        fill(xa_ref)
        compress(xb_ref)

    @pl.when(s % 2 == 1)
    def _():
        fill(xb_ref)
        compress(xa_ref)


def _compress_sample(pool_t, page_table, new_rows, cw):
    peb, w1b, b1b, w2b = cw
    B, n_pages = page_table.shape
    t = new_rows.shape[1]
    assert t <= CMP_STRIDE
    cpp = PAGE_SIZE // CMP_STRIDE
    n_pg = min(16, n_pages)
    assert n_pages % n_pg == 0
    tiles = n_pages // n_pg + 1
    steps = tiles + 1
    nh = N_KV * CMP_HIDDEN
    pe_s = peb.reshape(2, CMP_STRIDE, KV_DIM)
    w1_s = w1b.reshape(2, CMP_STRIDE, KV_DIM, nh)
    newc = jnp.pad(new_rows, ((0, 0), (0, CMP_STRIDE - t), (0, 0)))

    def page_map(b, s, pt, k):
        return (pt[b * n_pages + jnp.minimum(s * n_pg + k, n_pages - 1)], 0, 0)

    tc = n_pg * cpp
    grid_spec = pltpu.PrefetchScalarGridSpec(
        num_scalar_prefetch=1,
        grid=(B, steps),
        in_specs=[pl.BlockSpec((1, KV_DIM, PAGE_SIZE), functools.partial(page_map, k=k)) for k in range(n_pg)] + [
            pl.BlockSpec((1, CMP_STRIDE, KV_DIM), lambda b, s, pt: (b, 0, 0)),
            pl.BlockSpec((2, CMP_STRIDE, KV_DIM), lambda b, s, pt: (0, 0, 0)),
            pl.BlockSpec((2, CMP_STRIDE, KV_DIM, nh), lambda b, s, pt: (0, 0, 0, 0)),
            pl.BlockSpec((1, nh), lambda b, s, pt: (0, 0)),
            pl.BlockSpec((nh, KV_DIM), lambda b, s, pt: (0, 0))],
        out_specs=pl.BlockSpec((1, tc, KV_DIM), lambda b, s, pt: (b, jnp.maximum(s - 1, 0), 0)),
        scratch_shapes=[pltpu.VMEM((KV_DIM // 128, n_pg * PAGE_SIZE, 128), F32),
                        pltpu.VMEM((KV_DIM // 128, n_pg * PAGE_SIZE, 128), F32), pltpu.VMEM((8, nh), F32)],
    )
    return pl.pallas_call(
        functools.partial(_compress_sample_kernel, n_pg=n_pg),
        grid_spec=grid_spec,
        out_shape=jax.ShapeDtypeStruct((B, tiles * tc, KV_DIM), BF16),
        compiler_params=_cparams("parallel", "arbitrary"),
        name="nsa_compress_paged",
    )(page_table.reshape(-1), *([pool_t] * n_pg), newc, pe_s, w1_s, b1b, w2b)


def _nsa_attn_sample_kernel(pt_ref, q_ref, gate_ref, new_ref, *refs, n_pg, past, t_len, n_blk):
    kpages, vpages = refs[:n_pg], refs[n_pg:2 * n_pg]
    (kwb_ref, vwb_ref, kc_ref, vc_ref, c2s_ref, pick_ref, base_ref,
     o_ref, qs_ref, selb_ref, m_ref, l_ref, acc_ref, oo_ref) = refs[2 * n_pg:]
    s = pl.program_id(1)
    nh = HEADS_PER_KV
    rows = nh * N_KV * t_len
    nc = kc_ref.shape[1]
    nsp = c2s_ref.shape[1]
    hp = lax.Precision.HIGHEST
    nt_dims = (((1,), (1,)), ((), ()))
    r_iota = lax.broadcasted_iota(jnp.int32, (rows, 1), 0)
    qpos_r = past + r_iota % t_len
    lane_g = lax.broadcasted_iota(jnp.int32, (1, KV_DIM), 1) // HEAD_DIM
    row_g = (r_iota // t_len) % N_KV

    def tile_rows(x, k):
        return jnp.concatenate([x] * k, axis=0)

    @pl.when(s == 0)
    def _():
        qf = q_ref[...].astype(F32)
        q128 = jnp.concatenate([tile_rows(qf[:, KV_DIM * h:KV_DIM * (h + 1)], N_KV) for h in range(nh)], axis=0)
        q128 = jnp.where(lane_g == row_g, q128, 0.0).astype(BF16)
        qs_ref[...] = q128
        hd = N_HEADS * HEAD_DIM
        gx = [jnp.concatenate([tile_rows(gate_ref[:, br * hd + KV_DIM * h:br * hd + KV_DIM * (h + 1)].astype(F32), N_KV)
                               for h in range(nh)], axis=0) for br in range(3)]
        new = new_ref[...]
        n_idx = lax.broadcasted_iota(jnp.int32, (1, nc), 1)
        cmp_ok = (n_idx >= 1) & (n_idx <= n_blk) & (n_idx * CMP_STRIDE + (CMP_STRIDE - 1) <= qpos_r)
        s_c = lax.dot_general(q128, kc_ref[0], nt_dims, preferred_element_type=F32)
        p_c = _masked_softmax_rows(s_c, cmp_ok)
        o_c = jnp.dot(p_c.astype(BF16), vc_ref[0], preferred_element_type=F32)
        gq = N_KV * t_len
        pcs = p_c[0:gq]
        for h in range(1, nh):
            pcs = pcs + p_c[h * gq:(h + 1) * gq]
        imp = jnp.dot(pcs, c2s_ref[...], precision=hp, preferred_element_type=F32)
        cur = (past + lax.broadcasted_iota(jnp.int32, (gq, 1), 0) % t_len) // SEL_BLOCK
        sel = _select_blocks(imp, cur, N_SEL).astype(F32)
        selb = tile_rows(sel, nh).astype(BF16)
        for st in range(pick_ref.shape[0]):
            selb_ref[st] = jnp.dot(selb, pick_ref[st], preferred_element_type=F32).astype(BF16)
        kwin = jnp.concatenate([kwb_ref[0], new[:, 2 * KV_DIM:3 * KV_DIM]], axis=0).astype(BF16)
        vwin = jnp.concatenate([vwb_ref[0], new[:, 3 * KV_DIM:]], axis=0).astype(BF16)
        wb = kwb_ref.shape[1]
        kw_pos = past - wb + lax.broadcasted_iota(jnp.int32, (1, wb + t_len), 1)
        dlt = qpos_r - kw_pos
        win_ok = (dlt >= 0) & (dlt < WINDOW) & (kw_pos >= 0)
        s_w = lax.dot_general(q128, kwin, nt_dims, preferred_element_type=F32)
        p_w = _masked_softmax_rows(s_w, win_ok)
        o_w = jnp.dot(p_w.astype(BF16), vwin, preferred_element_type=F32)
        oo_ref[0] = gx[0] * o_c + gx[2] * o_w
        oo_ref[1] = gx[1]
        knew = new[:, 0:KV_DIM].astype(BF16)
        vnew = new[:, KV_DIM:2 * KV_DIM].astype(BF16)
        s_n = lax.dot_general(q128, knew, nt_dims, preferred_element_type=F32)
        jn = lax.broadcasted_iota(jnp.int32, (nsp, t_len), 0)
        picked = jnp.dot(selb, (jn == past // SEL_BLOCK).astype(BF16), preferred_element_type=F32)
        ok = (picked > 0.5) & (past + lax.broadcasted_iota(jnp.int32, (1, t_len), 1) <= qpos_r)
        s_n = jnp.where(ok, s_n, NEG_BIG)
        m0 = jnp.max(s_n, -1, keepdims=True)
        p0 = jnp.where(ok, jnp.exp2(s_n - m0), 0.0)
        m_ref[...] = m0
        l_ref[...] = jnp.sum(p0, -1, keepdims=True)
        acc_ref[...] = jnp.dot(p0.astype(BF16), vnew, preferred_element_type=F32)

    tk = n_pg * PAGE_SIZE
    qs = qs_ref[...]
    sc = jnp.concatenate([jnp.dot(qs, p[0].astype(BF16), preferred_element_type=F32) for p in kpages], axis=1)
    picked = jnp.dot(selb_ref[s], base_ref[...], preferred_element_type=F32)
    sc = jnp.where(picked > 0.5, sc, NEG_BIG)
    m = m_ref[...]
    m_new = jnp.maximum(m, jnp.max(sc, -1, keepdims=True))
    alpha = jnp.exp2(m - m_new)
    p = jnp.where(picked > 0.5, jnp.exp2(sc - m_new), 0.0)
    l_ref[...] = alpha * l_ref[...] + jnp.sum(p, -1, keepdims=True)
    pb = p.astype(BF16)
    pv = jnp.zeros((rows, KV_DIM), F32)
    for k, vp in enumerate(vpages):
        pv = pv + lax.dot_general(pb[:, k * PAGE_SIZE:(k + 1) * PAGE_SIZE], vp[0].astype(BF16), nt_dims,
                                  preferred_element_type=F32)
    acc_ref[...] = alpha * acc_ref[...] + pv
    m_ref[...] = m_new

    @pl.when(s == pl.num_programs(1) - 1)
    def _():
        l = l_ref[...]
        o_s = acc_ref[...] / jnp.where(l > 0, l, 1.0)
        total = oo_ref[0] + oo_ref[1] * o_s
        gq = N_KV * t_len
        for h in range(nh):
            oh = jnp.zeros((t_len, KV_DIM), F32)
            for g in range(N_KV):
                oh = jnp.where(lane_g == g, total[h * gq + g * t_len:h * gq + (g + 1) * t_len], oh)
            o_ref[:, KV_DIM * h:KV_DIM * (h + 1)] = oh.astype(o_ref.dtype)


def _nsa_attn_sample(q, gates, new, page_table, pool_ks, pool_vs, buf_kw, buf_vw, kc, vc, *, t_len):
    B, n_pages = page_table.shape
    past = n_pages * PAGE_SIZE
    total = past + t_len
    n_blk = -(-total // CMP_STRIDE) - 1
    ns = -(-total // SEL_BLOCK)
    nsp = -(-ns // 128) * 128
    nc = kc.shape[1]
    c2s = jnp.pad(_cmp_to_sel(nc, ns), ((0, 0), (0, nsp - ns)))
    n_pg = min(16, n_pages)
    assert n_pages % n_pg == 0 and t_len % 8 == 0
    steps = n_pages // n_pg
    rows = N_HEADS * t_len
    wb = buf_kw.shape[1]
    one = pl.Buffered(1)
    bps = n_pg * PAGE_SIZE // SEL_BLOCK
    jj = jnp.arange(nsp)[None, :, None]
    cc = jnp.arange(128)[None, None, :]
    pick = ((jj == jnp.arange(steps)[:, None, None] * bps + cc) & (cc < bps)).astype(BF16)
    base = (jnp.arange(128)[:, None] == jnp.arange(n_pg * PAGE_SIZE)[None, :] // SEL_BLOCK).astype(BF16)

    def page_map(b, s, pt, k):
        return (pt[b * n_pages + s * n_pg + k], 0, 0)

    per_b = lambda w: pl.BlockSpec((t_len, w), lambda b, s, pt: (b, 0))
    grid_spec = pltpu.PrefetchScalarGridSpec(
        num_scalar_prefetch=1,
        grid=(B, steps),
        in_specs=[per_b(D_MODEL), per_b(3 * D_MODEL), per_b(4 * KV_DIM)]
        + [pl.BlockSpec((1, KV_DIM, PAGE_SIZE), functools.partial(page_map, k=k)) for k in range(n_pg)] * 2
        + [pl.BlockSpec((1, wb, KV_DIM), lambda b, s, pt: (b, 0, 0)),
           pl.BlockSpec((1, wb, KV_DIM), lambda b, s, pt: (b, 0, 0)),
           pl.BlockSpec((1, nc, KV_DIM), lambda b, s, pt: (b, 0, 0)),
           pl.BlockSpec((1, nc, KV_DIM), lambda b, s, pt: (b, 0, 0)),
           pl.BlockSpec((nc, nsp), lambda b, s, pt: (0, 0), pipeline_mode=one),
           pl.BlockSpec((steps, nsp, 128), lambda b, s, pt: (0, 0, 0), pipeline_mode=one),
           pl.BlockSpec((128, n_pg * PAGE_SIZE), lambda b, s, pt: (0, 0), pipeline_mode=one)],
        out_specs=pl.BlockSpec((t_len, D_MODEL), lambda b, s, pt: (b, 0)),
        scratch_shapes=[pltpu.VMEM((rows, KV_DIM), BF16), pltpu.VMEM((steps, rows, 128), BF16),
                        pltpu.VMEM((rows, 1), F32), pltpu.VMEM((rows, 1), F32), pltpu.VMEM((rows, KV_DIM), F32),
                        pltpu.VMEM((2, rows, KV_DIM), F32)],
    )
    return pl.pallas_call(
        functools.partial(_nsa_attn_sample_kernel, n_pg=n_pg, past=past, t_len=t_len, n_blk=n_blk),
        grid_spec=grid_spec,
        out_shape=jax.ShapeDtypeStruct((B * t_len, D_MODEL), BF16),
        compiler_params=_cparams("parallel", "arbitrary"),
        name="nsa_attn_sample",
    )(page_table.reshape(-1), q, gates, new, *([pool_ks] * n_pg), *([pool_vs] * n_pg), buf_kw, buf_vw, kc, vc, c2s,
      pick, base)


def _nsa_sample_layer(x, page_table, pool_kc, pool_vc, pool_ks, pool_vs, buf_kw, buf_vw,
                      w_in, pe, w1, b1, w2, w_o, g, b, *, t_len):
    B = page_table.shape[0]
    q, k_c, v_c, _, gates, k_s, v_s, k_w, v_w = _nsa_proj(x, w_in, rows_t=False)
    flat = lambda p: p.reshape(p.shape[0], p.shape[1], KV_DIM)
    pages_t = lambda p: p.transpose(0, 2, 3, 1).reshape(p.shape[0], KV_DIM, p.shape[1])
    kc = _compress_sample(pages_t(pool_kc), page_table, k_c.reshape(B, t_len, KV_DIM), _cmp_weights(pe[0], w1[0], b1[0], w2[0]))
    vc = _compress_sample(pages_t(pool_vc), page_table, v_c.reshape(B, t_len, KV_DIM), _cmp_weights(pe[1], w1[1], b1[1], w2[1]))
    new = jnp.concatenate([k_s, v_s, k_w, v_w], axis=1)
    o = _nsa_attn_sample(q, gates, new, page_table, pages_t(pool_ks), pages_t(pool_vs), flat(buf_kw), flat(buf_vw),
                         kc, vc, t_len=t_len)
    y = _out_proj_ln(x, o, w_o, g, b)
    return y, (k_c, v_c, k_s, v_s, k_w, v_w), o


def kernel(x_prompt, x_sample, state_s5_re, state_s5_im, cache_k_cmp, cache_v_cmp, cache_k_sel, cache_v_sel,
           cache_k_win, cache_v_win, state_ffn_conv, page_table,
           s5_a_re, s5_a_im, s5_log_dt, s5_b_re, s5_b_im, s5_c_re, s5_c_im, s5_d, s5_w_glu,
           nsa_w_in, nsa_cmp_pe, nsa_cmp_w1, nsa_cmp_b1, nsa_cmp_w2, nsa_w_o,
           ffn_w_up, ffn_conv_w, ffn_conv_b, ffn_w_down,
           ln_mix_g, ln_mix_b, ln_ffn_g, ln_ffn_b):
    assert s5_a_re.shape[0] == 1 and nsa_w_in.shape[0] == 1 and ffn_w_up.shape[0] == DEPTH
    bsz, L, _ = x_prompt.shape
    B, t_len, _ = x_sample.shape
    xp, xs, st_p, st_s = _s5_layer(x_prompt, x_sample, state_s5_re[0], state_s5_im[0], s5_a_re[0], s5_a_im[0],
                                   s5_log_dt[0], s5_b_re[0], s5_b_im[0], s5_c_re[0], s5_c_im[0], s5_d[0],
                                   s5_w_glu[0], ln_mix_g[0], ln_mix_b[0])
    fw = (ffn_w_up[0], ffn_conv_w[0], ffn_conv_b[0], ffn_w_down[0], ln_ffn_g[0], ln_ffn_b[0])
    xp, cp0 = _ffn(xp, None, *fw, seq_len=L)
    xs, cs0 = _ffn(xs, state_ffn_conv[0], *fw, seq_len=t_len)
    nw = (nsa_w_in[0], nsa_cmp_pe[0], nsa_cmp_w1[0], nsa_cmp_b1[0], nsa_cmp_w2[0], nsa_w_o[0],
          ln_mix_g[1], ln_mix_b[1])
    xp, rows_p = _nsa_prompt_layer(xp, *nw)
    xs, rows_s, _ = _nsa_sample_layer(xs, page_table, cache_k_cmp[0], cache_v_cmp[0], cache_k_sel[0], cache_v_sel[0],
                                      cache_k_win[0], cache_v_win[0], *nw, t_len=t_len)
    fw = (ffn_w_up[1], ffn_conv_w[1], ffn_conv_b[1], ffn_w_down[1], ln_ffn_g[1], ln_ffn_b[1])
    xp, cp1 = _ffn(xp, None, *fw, seq_len=L)
    xs, cs1 = _ffn(xs, state_ffn_conv[1], *fw, seq_len=t_len)

    heads = lambda r, lead: r.reshape(1, lead, -1, N_KV, HEAD_DIM)
    heads_t = lambda r: r.reshape(N_KV, HEAD_DIM, -1).transpose(2, 0, 1)[None, None]
    win = min(WINDOW, L)
    wb = cache_k_win.shape[2]
    keep = min(WINDOW, wb + t_len)
    win_s = lambda buf, new: jnp.concatenate([buf[0], new.reshape(B, t_len, N_KV, HEAD_DIM)], axis=1)[None, :, wb + t_len - keep:]
    return (xp[None], xs.reshape(B, t_len, D_MODEL),
            st_p[0][None], st_p[1][None], st_s[0][None], st_s[1][None],
            heads_t(rows_p[0]), heads_t(rows_p[1]), heads_t(rows_p[2]), heads_t(rows_p[3]),
            heads(rows_s[0], B), heads(rows_s[1], B), heads(rows_s[2], B), heads(rows_s[3], B),
            heads_t(rows_p[4][:, L - win:]), heads_t(rows_p[5][:, L - win:]),
            win_s(cache_k_win, rows_s[4]), win_s(cache_v_win, rows_s[5]),
            jnp.stack([cp0, cp1]), jnp.stack([cs0, cs1]))


def _s5_layer(xp, xs, h_re, h_im, a_re, a_im, log_dt, b_re, b_im, c_re, c_im, d_skip, w_glu, g, b):
    assert xp.shape[0] == 1
    mats = _s5_prep(a_re, a_im, log_dt, b_re, b_im, c_re, c_im)
    P = S5_STATE
    x2p = xp[0]
    yp, hfp = _s5_core_prompt(x2p, mats)
    op = _s5_glu_ln(x2p, yp, d_skip, w_glu, g, b)
    x2s = xs.reshape(-1, D_MODEL)
    ys, hfs = _s5_core_sample(xs, h_re, h_im, mats)
    os_ = _s5_glu_ln(x2s, ys, d_skip, w_glu, g, b)
    st_p = (hfp[None, :, :P], hfp[None, :, P:])
    hfs = hfs.transpose(1, 0, 2)
    st_s = (hfs[:, :, :P], hfs[:, :, P:])
    return op, os_, st_p, st_s
```

```python
import functools
import math

import jax
import jax.numpy as jnp
from jax import lax
from jax.experimental import pallas as pl
from jax.experimental.pallas import tpu as pltpu

F32 = jnp.float32
BF16 = jnp.bfloat16

D_MODEL = 1024
S5_GROUP = 16
S5_GROUPS = D_MODEL // S5_GROUP
S5_STATE = 64
S5_T = 16
S5_SLAB = 128 // S5_GROUP
N_HEADS = 16
HEAD_DIM = 64
N_KV = 4
HEADS_PER_KV = N_HEADS // N_KV
KV_DIM = N_KV * HEAD_DIM
CMP_LEN = 32
CMP_STRIDE = 16
CMP_HIDDEN = 2 * HEAD_DIM
SEL_BLOCK = 64
N_SEL = 16
WINDOW = 512
PAGE_SIZE = 128
FORCE = 1e6
D_FF = 2816
CONV_W = 3
FFN_CHUNK = 2816
DEPTH = 2
ALPHA = (2.0 * DEPTH) ** 0.25
LN_EPS = 1e-5
NEG_BIG = -1e30

VMEM_LIMIT = 56 * 1024 * 1024


def _cparams(*sem):
    return pltpu.CompilerParams(dimension_semantics=sem, vmem_limit_bytes=VMEM_LIMIT)


def _layer_norm(r, g, b):
    mu = jnp.mean(r, -1, keepdims=True)
    c = r - mu
    var = jnp.mean(c * c, -1, keepdims=True)
    return c * lax.rsqrt(var + LN_EPS) * g + b


def _swap(x):
    return pltpu.roll(x, S5_STATE, axis=x.ndim - 1)


def _lo_mask(shape):
    return lax.broadcasted_iota(jnp.int32, shape, len(shape) - 1) < S5_STATE


def _cmul(x, w):
    ws = _swap(w)
    lo = _lo_mask(w.shape)
    wr = jnp.where(lo, w, ws)
    wi = jnp.where(lo, -ws, w)
    return x * wr + _swap(x) * wi


def _s5_prep_kernel(a_ref, dt_ref, bt_ref, btt_ref, ct_ref, m_ref, w_ref, v_ref, z_ref):
    P = S5_STATE
    T = S5_T
    a = a_ref[0]
    dt = jnp.exp(dt_ref[0])
    lo = _lo_mask(a.shape)
    a_sw = _swap(a)
    are = jnp.where(lo, a, a_sw)
    aim = jnp.where(lo, a_sw, a)
    mag = jnp.exp(are * dt)
    ang = aim * dt
    ab = mag * jnp.where(lo, jnp.cos(ang), jnp.sin(ang))
    ab_sw = _swap(ab)
    abr = jnp.where(lo, ab, ab_sw)
    abi = jnp.where(lo, ab_sw, ab)
    den = are * are + aim * aim
    nr = abr - 1.0
    f = jnp.where(lo, nr * are + abi * aim, abi * are - nr * aim) / den

    pw = [ab]
    for _ in range(4):
        pw.append(_cmul(pw[-1], pw[-1]))
    z_ref[0] = jnp.concatenate(
        [jnp.broadcast_to(pw[3], (4, 2 * P)), jnp.broadcast_to(pw[4], (4, 2 * P))], axis=0)

    rows = lax.broadcasted_iota(jnp.int32, (T * S5_GROUP, 2 * P), 0) // S5_GROUP
    e = (T - 1) - rows
    one = jnp.where(_lo_mask((T * S5_GROUP, 2 * P)), 1.0, 0.0).astype(F32)
    apow = one
    for b in range(4):
        apow = jnp.where(((e >> b) & 1) == 1, _cmul(apow, pw[b]), apow)
    af = _cmul(apow, f)
    w_ref[0] = _cmul(af, btt_ref[0])

    col = [jnp.transpose(jnp.broadcast_to(p, (2 * P, 2 * P))) for p in pw[:4]]
    lane_e = lax.broadcasted_iota(jnp.int32, (P, T * S5_GROUP), 1) // S5_GROUP

    def col_cmul(xr, xi, c):
        cr = jnp.concatenate([c[:P], c[:P]], axis=1)
        ci = jnp.concatenate([c[P:], c[P:]], axis=1)
        return xr * cr - xi * ci, xr * ci + xi * cr

    pr = jnp.ones((P, T * S5_GROUP), F32)
    pi = jnp.zeros((P, T * S5_GROUP), F32)
    for b in range(4):
        nr_, ni_ = col_cmul(pr, pi, col[b])
        sel = ((lane_e >> b) & 1) == 1
        pr = jnp.where(sel, nr_, pr)
        pi = jnp.where(sel, ni_, pi)
    ctr = ct_ref[0, 0]
    cti = ct_ref[0, 1]
    qr = pr * ctr - pi * cti
    qi = pr * cti + pi * ctr
    q1r, q1i = col_cmul(qr, qi, col[0])
    v_ref[0] = jnp.concatenate([q1r, -q1i], axis=0).astype(v_ref.dtype)

    fr = jnp.where(lo, f, _swap(f))[:, :P]
    fi = jnp.where(lo, _swap(f), f)[:, :P]
    btr = bt_ref[0, 0]
    bti = bt_ref[0, 1]
    bfr = btr * fr - bti * fi
    bfi = btr * fi + bti * fr
    hp = lax.Precision.HIGHEST
    kk = (jnp.dot(bfr, qr, precision=hp, preferred_element_type=F32)
          - jnp.dot(bfi, qi, precision=hp, preferred_element_type=F32))
    lane = lax.broadcasted_iota(jnp.int32, kk.shape, 1)
    for s in range(T):
        blk = kk if s == 0 else jnp.where(lane >= S5_GROUP * s, pltpu.roll(kk, S5_GROUP * s, axis=1), 0.0)
        m_ref[0, S5_GROUP * s:S5_GROUP * (s + 1), :] = blk.astype(m_ref.dtype)


def _s5_prep(a_re, a_im, log_dt, b_re, b_im, c_re, c_im):
    G, P, T = S5_GROUPS, S5_STATE, S5_T
    a_pk = jnp.concatenate([a_re, a_im], axis=-1).reshape(G, 1, 2 * P)
    dt_pk = jnp.broadcast_to(log_dt.reshape(G, 1, 1), (G, 1, 2 * P))
    bt = jnp.stack([b_re, b_im], axis=1).transpose(0, 1, 3, 2)
    btt = jnp.tile(jnp.concatenate([b_re, b_im], axis=1).transpose(0, 2, 1), (1, T, 1))
    ct = jnp.tile(jnp.stack([c_re, c_im], axis=1).transpose(0, 1, 3, 2), (1, 1, 1, T))
    n = T * S5_GROUP
    return pl.pallas_call(
        _s5_prep_kernel,
        grid=(G,),
        in_specs=[
            pl.BlockSpec((1, 1, 2 * P), lambda g: (g, 0, 0)),
            pl.BlockSpec((1, 1, 2 * P), lambda g: (g, 0, 0)),
            pl.BlockSpec((1, 2, S5_GROUP, P), lambda g: (g, 0, 0, 0)),
            pl.BlockSpec((1, n, 2 * P), lambda g: (g, 0, 0)),
            pl.BlockSpec((1, 2, P, n), lambda g: (g, 0, 0, 0)),
        ],
        out_specs=[
            pl.BlockSpec((1, n, n), lambda g: (g, 0, 0)),
            pl.BlockSpec((1, n, 2 * P), lambda g: (g, 0, 0)),
            pl.BlockSpec((1, 2 * P, n), lambda g: (g, 0, 0)),
            pl.BlockSpec((1, 8, 2 * P), lambda g: (g, 0, 0)),
        ],
        out_shape=[
            jax.ShapeDtypeStruct((G, n, n), BF16),
            jax.ShapeDtypeStruct((G, n, 2 * P), F32),
            jax.ShapeDtypeStruct((G, 2 * P, n), BF16),
            jax.ShapeDtypeStruct((G, 8, 2 * P), F32),
        ],
        compiler_params=_cparams("parallel"),
        name="s5_prep",
    )(a_pk, dt_pk, bt, btt, ct)


def _s5_scan_kernel(x_ref, m_ref, w_ref, v_ref, z_ref, y_ref, hf_ref, yg_ref):
    L = x_ref.shape[0]
    T, GS = S5_T, S5_GROUP
    C = L // T
    per = 128 // GS
    piece = lax.broadcasted_iota(jnp.int32, (1, 128), 1) // GS
    row = lax.broadcasted_iota(jnp.int32, (C, 128), 0)
    n_half = T // per

    def group(gp, carry):
        halves = []
        for hh in range(n_half):
            acc = None
            for tp in range(per):
                xt = x_ref[pl.ds(hh * per + tp, C, stride=T), :]
                r = pltpu.roll(xt, (GS * (tp - gp)) & 127, axis=1)
                acc = r if acc is None else jnp.where(piece == tp, r, acc)
            halves.append(acc)
        u = jnp.concatenate(halves, axis=1).astype(BF16)
        s = jnp.dot(u, w_ref[gp].astype(BF16), preferred_element_type=F32)
        z = z_ref[gp, 4:5, :]
        h = s
        d = 1
        while d < C:
            hs = jnp.where(row >= d, pltpu.roll(h, d, axis=0), 0.0)
            h = h + _cmul(hs, z)
            z = _cmul(z, z)
            d *= 2
        hprev = jnp.where(row >= 1, pltpu.roll(h, 1, axis=0), 0.0)
        y = (jnp.dot(u, m_ref[gp], preferred_element_type=F32)
             + jnp.dot(hprev.astype(BF16), v_ref[gp], preferred_element_type=F32))
        for hh in range(n_half):
            yg_ref[gp, hh] = y[:, 128 * hh:128 * (hh + 1)]
        hf_ref[gp] = h[C - 8:, :]
        return carry

    lax.fori_loop(0, S5_SLAB, group, 0)

    def position(t, carry):
        hh, tp = t // per, t % per
        acc = None
        for gp in range(S5_SLAB):
            r = pltpu.roll(yg_ref[gp, hh], (GS * (gp - tp)) & 127, axis=1)
            acc = r if acc is None else jnp.where(piece == gp, r, acc)
        y_ref[pl.ds(t, C, stride=T), :] = acc
        return carry

    lax.fori_loop(0, T, position, 0)


def _s5_step_kernel(u_ref, h0_ref, m_ref, w_ref, v_ref, z_ref, y_ref, hf_ref, *, t_len):
    n = t_len * S5_GROUP
    off = (S5_T - t_len) * S5_GROUP
    u = u_ref[0]
    h0 = h0_ref[0]
    hp = lax.Precision.HIGHEST
    s = jnp.dot(u, w_ref[0, off:, :], precision=hp, preferred_element_type=F32)
    z = z_ref[0, 0:1, :] if t_len * 2 == S5_T else z_ref[0, 4:5, :]
    hf_ref[0] = _cmul(h0, z) + s
    y = (jnp.dot(u.astype(BF16), m_ref[0, :n, :n], preferred_element_type=F32)
         + jnp.dot(h0.astype(BF16), v_ref[0, :, :n], preferred_element_type=F32))
    y_ref[0] = y.astype(y_ref.dtype)


def _s5_core_prompt(x, mats):
    m, w, v, z = mats
    L = x.shape[0]
    G, T = S5_GROUPS, S5_T
    C = L // T
    n = T * S5_GROUP
    gs = S5_SLAB
    assert C % 8 == 0 and S5_SLAB * S5_GROUP == 128
    y, hf = pl.pallas_call(
        _s5_scan_kernel,
        grid=(G // gs,),
        in_specs=[
            pl.BlockSpec((L, 128), lambda j: (0, j), pipeline_mode=pl.Buffered(1)),
            pl.BlockSpec((gs, n, n), lambda j: (j, 0, 0)),
            pl.BlockSpec((gs, n, 2 * S5_STATE), lambda j: (j, 0, 0)),
            pl.BlockSpec((gs, 2 * S5_STATE, n), lambda j: (j, 0, 0)),
            pl.BlockSpec((gs, 8, 2 * S5_STATE), lambda j: (j, 0, 0)),
        ],
        out_specs=[
            pl.BlockSpec((L, 128), lambda j: (0, j)),
            pl.BlockSpec((gs, 8, 2 * S5_STATE), lambda j: (j, 0, 0)),
        ],
        out_shape=[
            jax.ShapeDtypeStruct((L, D_MODEL), F32),
            jax.ShapeDtypeStruct((G, 8, 2 * S5_STATE), F32),
        ],
        scratch_shapes=[pltpu.VMEM((gs, n // 128, C, 128), F32)],
        compiler_params=_cparams("parallel"),
        name="s5_scan",
    )(x, m, w, v, z)
    return y, hf[:, 7, :]


def _s5_core_sample(x, h_re, h_im, mats):
    m, w, v, z = mats
    B, t_len, _ = x.shape
    assert t_len in (S5_T // 2, S5_T)
    G = S5_GROUPS
    n = t_len * S5_GROUP
    u = x.reshape(B, t_len, G, S5_GROUP).transpose(2, 0, 1, 3).reshape(G, B, n)
    h0 = jnp.concatenate([h_re, h_im], axis=-1).transpose(1, 0, 2)
    nf = S5_T * S5_GROUP
    y, hf = pl.pallas_call(
        functools.partial(_s5_step_kernel, t_len=t_len),
        grid=(G,),
        in_specs=[
            pl.BlockSpec((1, B, n), lambda g: (g, 0, 0)),
            pl.BlockSpec((1, B, 2 * S5_STATE), lambda g: (g, 0, 0)),
            pl.BlockSpec((1, nf, nf), lambda g: (g, 0, 0)),
            pl.BlockSpec((1, nf, 2 * S5_STATE), lambda g: (g, 0, 0)),
            pl.BlockSpec((1, 2 * S5_STATE, nf), lambda g: (g, 0, 0)),
            pl.BlockSpec((1, 8, 2 * S5_STATE), lambda g: (g, 0, 0)),
        ],
        out_specs=[
            pl.BlockSpec((1, B, n), lambda g: (g, 0, 0)),
            pl.BlockSpec((1, B, 2 * S5_STATE), lambda g: (g, 0, 0)),
        ],
        out_shape=[
            jax.ShapeDtypeStruct((G, B, n), BF16),
            jax.ShapeDtypeStruct((G, B, 2 * S5_STATE), F32),
        ],
        compiler_params=_cparams("parallel"),
        name="s5_step",
    )(u, h0, m, w, v, z)
    y = y.reshape(G, B, t_len, S5_GROUP).transpose(1, 2, 0, 3).reshape(B * t_len, D_MODEL)
    return y, hf


def _s5_glu_ln_kernel(x_ref, y_ref, d_ref, w_ref, g_ref, b_ref, o_ref):
    x = x_ref[...]
    yy = y_ref[...].astype(F32) + x * d_ref[...]
    zg = jax.nn.gelu(yy).astype(BF16)
    zz = jnp.dot(zg, w_ref[...], preferred_element_type=F32)
    out = zz[:, :D_MODEL] * jax.nn.sigmoid(zz[:, D_MODEL:])
    o_ref[...] = _layer_norm(ALPHA * x + out, g_ref[...], b_ref[...])


def _row_tile(n, cap):
    t = min(n, cap)
    while n % t:
        t //= 2
    return t


def _s5_glu_ln(x, y, d_skip, w_glu, g, b):
    n = x.shape[0]
    tm = _row_tile(n, 512)
    const = lambda i: (0, 0)
    return pl.pallas_call(
        _s5_glu_ln_kernel,
        grid=(n // tm,),
        in_specs=[
            pl.BlockSpec((tm, D_MODEL), lambda i: (i, 0)),
            pl.BlockSpec((tm, D_MODEL), lambda i: (i, 0)),
            pl.BlockSpec((1, D_MODEL), const),
            pl.BlockSpec((D_MODEL, 2 * D_MODEL), const),
            pl.BlockSpec((1, D_MODEL), const),
            pl.BlockSpec((1, D_MODEL), const),
        ],
        out_specs=pl.BlockSpec((tm, D_MODEL), lambda i: (i, 0)),
        out_shape=jax.ShapeDtypeStruct((n, D_MODEL), F32),
        compiler_params=_cparams("parallel"),
        name="s5_glu_ln",
    )(x, y, d_skip.reshape(1, -1), w_glu.astype(BF16), g.reshape(1, -1), b.reshape(1, -1))


def _ffn_kernel(x_ref, p1_ref, p2_ref, wu_ref, cw_ref, cb_ref, wd_ref, g_ref, b_ref,
                o_ref, st_ref, carry_ref, *, period, chained):
    i = pl.program_id(0)
    x = x_ref[...]
    xb = x.astype(BF16)
    tm = x.shape[0]
    row = lax.broadcasted_iota(jnp.int32, (tm, 1), 0)
    t = row % period
    if chained:
        @pl.when(i == 0)
        def _():
            carry_ref[...] = jnp.zeros_like(carry_ref)

    def conv(cols):
        h = jnp.dot(xb, wu_ref[:, cols], preferred_element_type=F32)
        h1 = jnp.where(t >= 1, pltpu.roll(h, 1, axis=0), 0.0)
        h2 = jnp.where(t >= 2, pltpu.roll(h, 2, axis=0), 0.0)
        if chained:
            c0 = carry_ref[6:7, cols]
            c1 = carry_ref[7:8, cols]
            h1 = h1 + jnp.where(row == 0, c1, 0.0)
            h2 = h2 + jnp.where(row == 0, c0, 0.0) + jnp.where(row == 1, c1, 0.0)
            carry_ref[:, cols] = h[tm - 8:, :]
            st_ref[:, cols] = h[tm - 8:, :]
        else:
            h1 = h1 + p1_ref[:, cols]
            h2 = h2 + p2_ref[:, cols]
            st_ref[:, cols] = h
        return cb_ref[:, cols] + cw_ref[0:1, cols] * h2 + cw_ref[1:2, cols] * h1 + cw_ref[2:3, cols] * h

    f = jnp.zeros((tm, D_MODEL), F32)
    for c in range(D_FF // FFN_CHUNK):
        ca = slice(c * FFN_CHUNK, (c + 1) * FFN_CHUNK)
        cv = slice(D_FF + c * FFN_CHUNK, D_FF + (c + 1) * FFN_CHUNK)
        act = jax.nn.silu(conv(ca)) * conv(cv)
        f = f + jnp.dot(act.astype(BF16), wd_ref[ca, :], preferred_element_type=F32)
    o_ref[...] = _layer_norm(ALPHA * x + f, g_ref[...], b_ref[...])


def _ffn(x, buf, w_up, conv_w, conv_b, w_down, g, b, *, seq_len):
    n = x.shape[0]
    n_seq = n // seq_len
    f2 = 2 * D_FF
    chained = buf is None
    if chained:
        assert n_seq == 1
        tm = _row_tile(n, 512)
        p1 = p2 = jnp.zeros((8, f2), F32)
        pspec = pl.BlockSpec((8, f2), lambda i: (0, 0))
        st_shape, st_spec = (8, f2), pl.BlockSpec((8, f2), lambda i: (0, 0))
    else:
        tm = n
        z = jnp.zeros((n_seq, seq_len, f2), F32)
        p1 = z.at[:, 0].set(buf[:, 1]).reshape(n, f2)
        p2 = z.at[:, 0].set(buf[:, 0]).at[:, 1].set(buf[:, 1]).reshape(n, f2)
        pspec = pl.BlockSpec((tm, f2), lambda i: (i, 0))
        st_shape, st_spec = (n, f2), pl.BlockSpec((tm, f2), lambda i: (i, 0))
    const = lambda i: (0, 0)
    out, st = pl.pallas_call(
        functools.partial(_ffn_kernel, period=seq_len, chained=chained),
        grid=(n // tm,),
        in_specs=[
            pl.BlockSpec((tm, D_MODEL), lambda i: (i, 0)),
            pspec, pspec,
            pl.BlockSpec((D_MODEL, f2), const, pipeline_mode=pl.Buffered(1)),
            pl.BlockSpec((CONV_W, f2), const),
            pl.BlockSpec((1, f2), const),
            pl.BlockSpec((D_FF, D_MODEL), const, pipeline_mode=pl.Buffered(1)),
            pl.BlockSpec((1, D_MODEL), const),
            pl.BlockSpec((1, D_MODEL), const),
        ],
        out_specs=[pl.BlockSpec((tm, D_MODEL), lambda i: (i, 0)), st_spec],
        out_shape=[jax.ShapeDtypeStruct((n, D_MODEL), F32), jax.ShapeDtypeStruct(st_shape, F32)],
        scratch_shapes=[pltpu.VMEM((8, f2), F32)],
        compiler_params=_cparams("arbitrary"),
        name="conv_ffn",
    )(x, p1, p2, w_up.astype(BF16), conv_w, conv_b.reshape(1, -1), w_down.astype(BF16),
      g.reshape(1, -1), b.reshape(1, -1))
    if chained:
        state = st[6:8][None]
    else:
        state = st.reshape(n_seq, seq_len, f2)[:, seq_len - 2:]
    return out, state


Q_SCALE = HEAD_DIM ** -0.5 * math.log2(math.e)
TQ = 128
TK = 512


def _perm_heads():
    idx = jnp.arange(D_MODEL).reshape(N_KV, HEADS_PER_KV, HEAD_DIM)
    return idx.transpose(1, 0, 2).reshape(-1)


def _nsa_proj_kernel(x_ref, wq_ref, wkv_ref, wg_ref, *refs, rows_t):
    xb = x_ref[...].astype(BF16)
    q = jnp.dot(xb, wq_ref[...], preferred_element_type=F32) * Q_SCALE
    kv = jnp.dot(xb, wkv_ref[...], preferred_element_type=F32)
    gates = jax.nn.sigmoid(jnp.dot(xb, wg_ref[...], preferred_element_type=F32))
    if rows_t:
        wkvt_ref, q_ref, kc_ref, vc_ref, kvb_ref, g_ref = refs[:6]
        kvt = lax.dot_general(wkvt_ref[...], xb, (((1,), (1,)), ((), ())), preferred_element_type=F32)
        for j, r in enumerate(refs[6:]):
            r[...] = kvt[j * KV_DIM:(j + 1) * KV_DIM, :]
    else:
        q_ref, kc_ref, vc_ref, kvb_ref, g_ref = refs[:5]
        for j, r in enumerate(refs[5:]):
            r[...] = kv[:, (j + 2) * KV_DIM:(j + 3) * KV_DIM]
    q_ref[...] = q.astype(BF16)
    kc_ref[...] = kv[:, :KV_DIM]
    vc_ref[...] = kv[:, KV_DIM:2 * KV_DIM]
    kvb_ref[...] = kv[:, 2 * KV_DIM:].astype(BF16)
    g_ref[...] = gates.astype(BF16)


def _gate_columns():
    br = jnp.arange(3)[:, None, None, None]
    h = jnp.arange(HEADS_PER_KV)[None, :, None, None]
    g = jnp.arange(N_KV)[None, None, :, None]
    idx = (g * HEADS_PER_KV + h) * 3 + br + jnp.zeros((1, 1, 1, HEAD_DIM), jnp.int32)
    return idx.reshape(-1)


def _nsa_proj(x, w_in, *, rows_t):
    n = x.shape[0]
    tm = _row_tile(n, 512)
    hd = N_HEADS * HEAD_DIM
    wq = w_in[:, :hd][:, _perm_heads()].astype(BF16)
    wkv = w_in[:, hd:hd + 6 * KV_DIM].astype(BF16)
    wg = w_in[:, hd + 6 * KV_DIM:][:, _gate_columns()].astype(BF16)
    const = lambda i: (0, 0)
    rows = lambda w: pl.BlockSpec((tm, w), lambda i: (i, 0))
    in_specs = [rows(D_MODEL), pl.BlockSpec((D_MODEL, hd), const), pl.BlockSpec((D_MODEL, 6 * KV_DIM), const),
                pl.BlockSpec((D_MODEL, 3 * hd), const)]
    out_specs = [rows(hd), rows(KV_DIM), rows(KV_DIM), rows(4 * KV_DIM), rows(3 * hd)]
    out_shape = [jax.ShapeDtypeStruct((n, hd), BF16), jax.ShapeDtypeStruct((n, KV_DIM), F32),
                 jax.ShapeDtypeStruct((n, KV_DIM), F32), jax.ShapeDtypeStruct((n, 4 * KV_DIM), BF16),
                 jax.ShapeDtypeStruct((n, 3 * hd), BF16)]
    args = [x, wq, wkv, wg]
    if rows_t:
        in_specs.append(pl.BlockSpec((6 * KV_DIM, D_MODEL), const))
        args.append(wkv.T)
        out_specs += [pl.BlockSpec((KV_DIM, tm), lambda i: (0, i))] * 6
        out_shape += [jax.ShapeDtypeStruct((KV_DIM, n), F32)] * 6
    else:
        out_specs += [rows(KV_DIM)] * 4
        out_shape += [jax.ShapeDtypeStruct((n, KV_DIM), F32)] * 4
    return pl.pallas_call(
        functools.partial(_nsa_proj_kernel, rows_t=rows_t),
        grid=(n // tm,),
        in_specs=in_specs,
        out_specs=out_specs,
        out_shape=out_shape,
        compiler_params=_cparams("parallel"),
        name="nsa_proj",
    )(*args)


def _cmp_weights(pe, w1, b1, w2):
    eye = jnp.eye(N_KV, dtype=F32)
    w1b = jnp.einsum('jsdh,ge->jsgdeh', w1.reshape(2, CMP_STRIDE, HEAD_DIM, CMP_HIDDEN), eye)
    w1b = w1b.reshape(2, CMP_STRIDE * KV_DIM, N_KV * CMP_HIDDEN).astype(BF16)
    w2b = jnp.einsum('hd,ge->ghed', w2, eye).reshape(N_KV * CMP_HIDDEN, KV_DIM).astype(BF16)
    peb = jnp.broadcast_to(pe.reshape(2, CMP_STRIDE, 1, HEAD_DIM), (2, CMP_STRIDE, N_KV, HEAD_DIM))
    peb = peb.reshape(2, 1, CMP_STRIDE * KV_DIM)
    b1b = jnp.tile(b1, N_KV).reshape(1, N_KV * CMP_HIDDEN)
    return peb, w1b, b1b, w2b


def _compress_rows(x, carry_ref, pe_ref, w1_ref, b1_ref, w2_ref):
    p0 = jnp.dot((x + pe_ref[0]).astype(BF16), w1_ref[0], preferred_element_type=F32)
    p1 = jnp.dot((x + pe_ref[1]).astype(BF16), w1_ref[1], preferred_element_type=F32)
    row = lax.broadcasted_iota(jnp.int32, (x.shape[0], 1), 0)
    p0s = jnp.where(row == 0, carry_ref[7:8, :], pltpu.roll(p0, 1, axis=0))
    carry_ref[...] = p0[x.shape[0] - 8:, :]
    h = b1_ref[...] + p0s + p1
    return jnp.dot(jax.nn.gelu(h).astype(BF16), w2_ref[...], preferred_element_type=F32)


def _compress_prompt_kernel(x_ref, pe_ref, w1_ref, b1_ref, w2_ref, o_ref, carry_ref):
    @pl.when(pl.program_id(0) == 0)
    def _():
        carry_ref[...] = jnp.zeros_like(carry_ref)
    o_ref[...] = _compress_rows(x_ref[...], carry_ref, pe_ref, w1_ref, b1_ref, w2_ref).astype(o_ref.dtype)


def _compress_prompt(rows, cw):
    peb, w1b, b1b, w2b = cw
    n = rows.shape[0]
    nc = n // CMP_STRIDE
    wide = CMP_STRIDE * KV_DIM
    x = rows.reshape(nc, wide)
    tc = _row_tile(nc, 256)
    return pl.pallas_call(
        _compress_prompt_kernel,
        grid=(nc // tc,),
        in_specs=[pl.BlockSpec((tc, wide), lambda i: (i, 0)),
                  pl.BlockSpec((2, 1, wide), lambda i: (0, 0, 0)),
                  pl.BlockSpec((2, wide, N_KV * CMP_HIDDEN), lambda i: (0, 0, 0)),
                  pl.BlockSpec((1, N_KV * CMP_HIDDEN), lambda i: (0, 0)),
                  pl.BlockSpec((N_KV * CMP_HIDDEN, KV_DIM), lambda i: (0, 0))],
        out_specs=pl.BlockSpec((tc, KV_DIM), lambda i: (i, 0)),
        out_shape=jax.ShapeDtypeStruct((nc, KV_DIM), BF16),
        scratch_shapes=[pltpu.VMEM((8, N_KV * CMP_HIDDEN), F32)],
        compiler_params=_cparams("arbitrary"),
        name="nsa_compress",
    )(x, peb, w1b, b1b, w2b)


def _masked_softmax_rows(s, valid):
    s = jnp.where(valid, s, NEG_BIG)
    m = jnp.max(s, -1, keepdims=True)
    e = jnp.where(valid, jnp.exp2(s - m), 0.0)
    den = jnp.sum(e, -1, keepdims=True)
    return e / jnp.where(den > 0, den, 1.0)


def _biased_softmax_rows(s, bias, row_ok):
    s = s + bias
    m = jnp.max(s, -1, keepdims=True)
    e = jnp.exp2(s - m)
    den = jnp.sum(e, -1, keepdims=True)
    return e * jnp.where(row_ok, 1.0 / den, 0.0)


def _group_lanes(per_group):
    lo = lax.broadcasted_iota(jnp.int32, (1, 128), 1) < HEAD_DIM
    return jnp.concatenate([jnp.where(lo, per_group[2 * c], per_group[2 * c + 1]) for c in range(N_KV // 2)], axis=1)


def _select_blocks(imp, cur, n_sel, axis=1, three_forced=False):
    ns = imp.shape[axis]
    jj = lax.broadcasted_iota(jnp.int32, (1, ns) if axis == 1 else (ns, 1), axis)
    forced = (jj == 0) | ((jj <= cur) & (jj >= cur - 1))
    if three_forced and ns >= n_sel:
        v = jnp.where(jj > cur, -FORCE, jnp.where(forced, -jnp.inf, imp))
        n_sel = n_sel - 3
    else:
        v = jnp.where(jj > cur, -FORCE, jnp.where(forced, FORCE, imp))
    for _ in range(min(n_sel, ns)):
        m = jnp.max(v, axis, keepdims=True)
        idx = jnp.min(jnp.where(v == m, jj, ns), axis, keepdims=True)
        v = jnp.where(jj == idx, -jnp.inf, v)
    return v == -jnp.inf


def _nsa_attn_prompt_kernel(q_ref, gate_ref, ks_ref, vs_ref, *rest):
    nwb = WINDOW // TQ + 1
    kw_refs, vw_refs = rest[:nwb], rest[nwb:2 * nwb]
    kc_ref, vc_ref, c2st_ref, o_ref, qg_ref, selb_ref, m_ref, l_ref, acc_ref, oo_ref = rest[2 * nwb:]
    i = pl.program_id(0)
    nh = HEADS_PER_KV
    nc = kc_ref.shape[0]
    ns = c2st_ref.shape[0]
    rows = nh * TQ
    hd = N_HEADS * HEAD_DIM
    q = q_ref[...]
    qh = jnp.concatenate([q[:, KV_DIM * h:KV_DIM * (h + 1)] for h in range(nh)], axis=0)
    lane_g = lax.broadcasted_iota(jnp.int32, (1, KV_DIM), 1) // HEAD_DIM
    qpos_r = i * TQ + lax.broadcasted_iota(jnp.int32, (rows, 1), 0) % TQ
    qpos_q = i * TQ + lax.broadcasted_iota(jnp.int32, (TQ, 1), 0)
    cur_t = (i * TQ + lax.broadcasted_iota(jnp.int32, (1, TQ), 1)) // SEL_BLOCK
    nt = (((i + 1) * TQ + TK - 1) // TK)
    nt_dims = (((1,), (1,)), ((), ()))

    def gate_rows(br):
        return jnp.concatenate([gate_ref[:, br * hd + KV_DIM * h:br * hd + KV_DIM * (h + 1)] for h in range(nh)],
                               axis=0).astype(F32)

    kwin = jnp.concatenate([r[...] for r in kw_refs], axis=0)
    vwin = jnp.concatenate([r[...] for r in vw_refs], axis=0)
    kw_pos = i * TQ - WINDOW + lax.broadcasted_iota(jnp.int32, (1, WINDOW + TQ), 1)
    dlt = qpos_r - kw_pos
    bias_w = jnp.where((dlt >= 0) & (dlt < WINDOW) & (kw_pos >= 0), 0.0, NEG_BIG)
    n_idx = lax.broadcasted_iota(jnp.int32, (1, nc), 1)
    bias_c = jnp.where((n_idx >= 1) & (n_idx * CMP_STRIDE + (CMP_STRIDE - 1) <= qpos_r), 0.0, NEG_BIG)
    row_ok_c = qpos_r >= 2 * CMP_STRIDE - 1
    g_c, g_w = gate_rows(0), gate_rows(2)

    impt = []
    for g in range(N_KV):
        qg = jnp.where(lane_g == g, qh, jnp.zeros_like(qh))
        qg_ref[g * rows:(g + 1) * rows, :] = qg
        s_c = lax.dot_general(qg, kc_ref[...], nt_dims, preferred_element_type=F32)
        p_c = _biased_softmax_rows(s_c, bias_c, row_ok_c)
        o_c = jnp.dot(p_c.astype(BF16), vc_ref[...], preferred_element_type=F32)
        pcs = p_c[0:TQ]
        for h in range(1, nh):
            pcs = pcs + p_c[h * TQ:(h + 1) * TQ]
        acc_i = jnp.zeros((ns, TQ), F32)
        rem = pcs
        for _ in range(3):
            part = rem.astype(BF16)
            rem = rem - part.astype(F32)
            acc_i = acc_i + lax.dot_general(c2st_ref[...], part, nt_dims, preferred_element_type=F32)
        impt.append(acc_i)
        s_w = lax.dot_general(qg, kwin, nt_dims, preferred_element_type=F32)
        p_w = _biased_softmax_rows(s_w, bias_w, True)
        o_w = jnp.dot(p_w.astype(BF16), vwin, preferred_element_type=F32)
        oo = jnp.where(lane_g == g, g_c * o_c + g_w * o_w, oo) if g else g_c * o_c + g_w * o_w
        m_ref[g] = jnp.full((rows, 128), NEG_BIG, F32)
        l_ref[g] = jnp.zeros((rows, 128), F32)
    oo_ref[...] = oo
    acc_ref[...] = jnp.zeros((rows, KV_DIM), F32)
    sel_t = _select_blocks(jnp.concatenate(impt, axis=1), jnp.concatenate([cur_t] * N_KV, axis=1), N_SEL, axis=0)
    sel_t = sel_t.astype(F32)
    for g in range(N_KV):
        selb_ref[g * TQ:(g + 1) * TQ, :] = jnp.transpose(sel_t[:, g * TQ:(g + 1) * TQ]).astype(BF16)

    j_row = lax.broadcasted_iota(jnp.int32, (ns, TK), 0)
    j_of_lane = lax.broadcasted_iota(jnp.int32, (ns, TK), 1) // SEL_BLOCK
    key_lane = lax.broadcasted_iota(jnp.int32, (1, TK), 1)

    def body(kt, carry):
        k0 = pl.multiple_of(kt * TK, TK)
        kblk = ks_ref[pl.ds(k0, TK), :]
        vblk = vs_ref[pl.ds(k0, TK), :]
        onehot = (j_row == kt * (TK // SEL_BLOCK) + j_of_lane).astype(BF16)
        causal = k0 + key_lane <= qpos_q
        alphas = []
        pv = None
        for g in range(N_KV):
            picked = jnp.dot(selb_ref[g * TQ:(g + 1) * TQ, :], onehot, preferred_element_type=F32)
            bias = jnp.where((picked > 0.5) & causal, 0.0, NEG_BIG)
            s = lax.dot_general(qg_ref[g * rows:(g + 1) * rows, :], kblk, nt_dims, preferred_element_type=F32)
            s = s + jnp.concatenate([bias] * nh, axis=0)
            m_prev = m_ref[g]
            m_new = jnp.maximum(m_prev, jnp.max(s, -1, keepdims=True))
            alpha = jnp.exp2(m_prev - m_new)
            p = jnp.exp2(s - jnp.concatenate([m_new] * (TK // 128), axis=1))
            l_ref[g] = alpha * l_ref[g] + jnp.sum(p, -1, keepdims=True)
            m_ref[g] = m_new
            alphas.append(alpha)
            pv_g = jnp.dot(p.astype(BF16), jnp.where(lane_g == g, vblk, jnp.zeros_like(vblk)),
                           preferred_element_type=F32)
            pv = pv_g if pv is None else pv + pv_g
        acc_ref[...] = _group_lanes(alphas) * acc_ref[...] + pv
        return carry

    lax.fori_loop(0, (nt + 1) // 2, lambda kk, c: body(2 * kk + 1, body(2 * kk, c)), 0)

    o_s = acc_ref[...] / _group_lanes([l_ref[g] for g in range(N_KV)])
    total = oo_ref[...] + gate_rows(1) * o_s
    for h in range(nh):
        o_ref[:, KV_DIM * h:KV_DIM * (h + 1)] = total[h * TQ:(h + 1) * TQ].astype(o_ref.dtype)


def _cmp_to_sel(nc, ns):
    blk = jnp.arange(nc)[:, None] - 1
    i = blk * CMP_STRIDE
    j = jnp.arange(ns)[None, :] * SEL_BLOCK
    return ((blk >= 0) & (i < j + SEL_BLOCK) & (i + CMP_LEN > j)).astype(F32)


def _nsa_attn_prompt(q, gates, kvb, kc, vc):
    n = q.shape[0]
    nc, ns = n // CMP_STRIDE, n // SEL_BLOCK
    assert n % TK == 0 and WINDOW % TQ == 0
    wpad = jnp.pad(kvb[:, 2 * KV_DIM:], ((WINDOW, 0), (0, 0)))
    c2st = _cmp_to_sel(nc, ns).T.astype(BF16)
    nwb = WINDOW // TQ + 1
    rows = HEADS_PER_KV * TQ
    const2 = lambda i: (0, 0)
    one = pl.Buffered(1)
    in_specs = [
        pl.BlockSpec((TQ, D_MODEL), lambda i: (i, 0)),
        pl.BlockSpec((TQ, 3 * D_MODEL), lambda i: (i, 0)),
        pl.BlockSpec((n, KV_DIM), lambda i: (0, 0), pipeline_mode=one),
        pl.BlockSpec((n, KV_DIM), lambda i: (0, 1), pipeline_mode=one),
    ]
    in_specs += [pl.BlockSpec((TQ, KV_DIM), functools.partial(lambda i, k: (i + k, 0), k=k)) for k in range(nwb)]
    in_specs += [pl.BlockSpec((TQ, KV_DIM), functools.partial(lambda i, k: (i + k, 1), k=k)) for k in range(nwb)]
    in_specs += [
        pl.BlockSpec((nc, KV_DIM), const2, pipeline_mode=one),
        pl.BlockSpec((nc, KV_DIM), const2, pipeline_mode=one),
        pl.BlockSpec((ns, nc), const2, pipeline_mode=one),
    ]
    return pl.pallas_call(
        _nsa_attn_prompt_kernel,
        grid=(n // TQ,),
        in_specs=in_specs,
        out_specs=pl.BlockSpec((TQ, D_MODEL), lambda i: (i, 0)),
        out_shape=jax.ShapeDtypeStruct((n, D_MODEL), BF16),
        scratch_shapes=[pltpu.VMEM((N_KV * rows, KV_DIM), BF16), pltpu.VMEM((N_KV * TQ, ns), BF16),
                        pltpu.VMEM((N_KV, rows, 128), F32), pltpu.VMEM((N_KV, rows, 128), F32),
                        pltpu.VMEM((rows, KV_DIM), F32), pltpu.VMEM((rows, KV_DIM), F32)],
        compiler_params=_cparams("parallel"),
        name="nsa_attn_prompt",
    )(q, gates, kvb, kvb, *([wpad] * (2 * nwb)), kc, vc, c2st)


def _out_proj_ln_kernel(x_ref, o_ref, w_ref, g_ref, b_ref, y_ref):
    y = jnp.dot(o_ref[...], w_ref[...], preferred_element_type=F32)
    y_ref[...] = _layer_norm(ALPHA * x_ref[...] + y, g_ref[...], b_ref[...])


def _out_proj_ln(x, o, w_o, g, b):
    n = x.shape[0]
    tm = _row_tile(n, 512)
    const = lambda i: (0, 0)
    return pl.pallas_call(
        _out_proj_ln_kernel,
        grid=(n // tm,),
        in_specs=[pl.BlockSpec((tm, D_MODEL), lambda i: (i, 0)), pl.BlockSpec((tm, D_MODEL), lambda i: (i, 0)),
                  pl.BlockSpec((D_MODEL, D_MODEL), const), pl.BlockSpec((1, D_MODEL), const),
                  pl.BlockSpec((1, D_MODEL), const)],
        out_specs=pl.BlockSpec((tm, D_MODEL), lambda i: (i, 0)),
        out_shape=jax.ShapeDtypeStruct((n, D_MODEL), F32),
        compiler_params=_cparams("parallel"),
        name="nsa_out_ln",
    )(x, o, w_o[_perm_heads(), :].astype(BF16), g.reshape(1, -1), b.reshape(1, -1))


def _nsa_prompt_layer(x, w_in, pe, w1, b1, w2, w_o, g, b):
    q, k_c, v_c, kvb, gates, *rows_t = _nsa_proj(x, w_in, rows_t=True)
    kc = _compress_prompt(k_c, _cmp_weights(pe[0], w1[0], b1[0], w2[0]))
    vc = _compress_prompt(v_c, _cmp_weights(pe[1], w1[1], b1[1], w2[1]))
    o = _nsa_attn_prompt(q, gates, kvb, kc, vc)
    y = _out_proj_ln(x, o, w_o, g, b)
    return y, tuple(rows_t)


def _compress_sample_kernel(pt_ref, *refs, n_pg):
    pages = refs[:n_pg]
    new_ref, pe_ref, w1_ref, b1_ref, w2_ref, o_ref, xa_ref, xb_ref, carry_ref = refs[n_pg:]
    s = pl.program_id(1)
    new_tile = pl.num_programs(1) - 2
    tc = n_pg * (PAGE_SIZE // CMP_STRIDE)
    nl = KV_DIM // 128
    nh = N_KV * CMP_HIDDEN

    @pl.when(s == 0)
    def _():
        xa_ref[...] = jnp.zeros_like(xa_ref)
        xb_ref[...] = jnp.zeros_like(xb_ref)

    @pl.when(s == 1)
    def _():
        carry_ref[...] = jnp.zeros_like(carry_ref)

    def fill(dst_ref):
        for k, p in enumerate(pages):
            xt = jnp.transpose(p[0])
            if k == 0:
                head = jnp.where(s == new_tile, new_ref[0], xt[:CMP_STRIDE])
                xt = jnp.concatenate([head, xt[CMP_STRIDE:]], axis=0)
            for c in range(nl):
                dst_ref[c, k * PAGE_SIZE:(k + 1) * PAGE_SIZE, :] = xt[:, c * 128:(c + 1) * 128]

    def compress(src_ref):
        p0 = jnp.zeros((tc, nh), F32)
        p1 = jnp.zeros((tc, nh), F32)
        for ss in range(CMP_STRIDE):
            x = jnp.concatenate([src_ref[c, pl.ds(ss, tc, stride=CMP_STRIDE), :] for c in range(nl)], axis=1)
            p0 = p0 + jnp.dot((x + pe_ref[0, ss:ss + 1, :]).astype(BF16), w1_ref[0, ss], preferred_element_type=F32)
            p1 = p1 + jnp.dot((x + pe_ref[1, ss:ss + 1, :]).astype(BF16), w1_ref[1, ss], preferred_element_type=F32)
        row = lax.broadcasted_iota(jnp.int32, (tc, 1), 0)
        p0s = jnp.where(row == 0, carry_ref[7:8, :], pltpu.roll(p0, 1, axis=0))
        carry_ref[...] = p0[tc - 8:, :]
        h = b1_ref[...] + p0s + p1
        o_ref[0] = jnp.dot(jax.nn.gelu(h).astype(BF16), w2_ref[...], preferred_element_type=F32).astype(o_ref.dtype)

    @pl.when(s % 2 == 0)
    def _():
        fill(xa_ref)
        compress(xb_ref)

    @pl.when(s % 2 == 1)
    def _():
        fill(xb_ref)
        compress(xa_ref)


def _compress_sample(pool_t, page_table, new_rows, cw):
    peb, w1b, b1b, w2b = cw
    B, n_pages = page_table.shape
    t = new_rows.shape[1]
    assert t <= CMP_STRIDE
    cpp = PAGE_SIZE // CMP_STRIDE
    n_pg = min(16, n_pages)
    assert n_pages % n_pg == 0
    tiles = n_pages // n_pg + 1
    steps = tiles + 1
    nh = N_KV * CMP_HIDDEN
    pe_s = peb.reshape(2, CMP_STRIDE, KV_DIM)
    w1_s = w1b.reshape(2, CMP_STRIDE, KV_DIM, nh)
    newc = jnp.pad(new_rows, ((0, 0), (0, CMP_STRIDE - t), (0, 0)))

    def page_map(b, s, pt, k):
        return (pt[b * n_pages + jnp.minimum(s * n_pg + k, n_pages - 1)], 0, 0)

    tc = n_pg * cpp
    grid_spec = pltpu.PrefetchScalarGridSpec(
        num_scalar_prefetch=1,
        grid=(B, steps),
        in_specs=[pl.BlockSpec((1, KV_DIM, PAGE_SIZE), functools.partial(page_map, k=k)) for k in range(n_pg)] + [
            pl.BlockSpec((1, CMP_STRIDE, KV_DIM), lambda b, s, pt: (b, 0, 0)),
            pl.BlockSpec((2, CMP_STRIDE, KV_DIM), lambda b, s, pt: (0, 0, 0)),
            pl.BlockSpec((2, CMP_STRIDE, KV_DIM, nh), lambda b, s, pt: (0, 0, 0, 0)),
            pl.BlockSpec((1, nh), lambda b, s, pt: (0, 0)),
            pl.BlockSpec((nh, KV_DIM), lambda b, s, pt: (0, 0))],
        out_specs=pl.BlockSpec((1, tc, KV_DIM), lambda b, s, pt: (b, jnp.maximum(s - 1, 0), 0)),
        scratch_shapes=[pltpu.VMEM((KV_DIM // 128, n_pg * PAGE_SIZE, 128), F32),
                        pltpu.VMEM((KV_DIM // 128, n_pg * PAGE_SIZE, 128), F32), pltpu.VMEM((8, nh), F32)],
    )
    return pl.pallas_call(
        functools.partial(_compress_sample_kernel, n_pg=n_pg),
        grid_spec=grid_spec,
        out_shape=jax.ShapeDtypeStruct((B, tiles * tc, KV_DIM), BF16),
        compiler_params=_cparams("parallel", "arbitrary"),
        name="nsa_compress_paged",
    )(page_table.reshape(-1), *([pool_t] * n_pg), newc, pe_s, w1_s, b1b, w2b)


def _nsa_attn_sample_kernel(pt_ref, q_ref, gate_ref, new_ref, *refs, n_pg, past, t_len, n_blk):
    kpages, vpages = refs[:n_pg], refs[n_pg:2 * n_pg]
    (kwb_ref, vwb_ref, kc_ref, vc_ref, c2s_ref, pick_ref, base_ref,
     o_ref, qs_ref, selb_ref, m_ref, l_ref, acc_ref, oo_ref) = refs[2 * n_pg:]
    s = pl.program_id(1)
    nh = HEADS_PER_KV
    rows = nh * N_KV * t_len
    nc = kc_ref.shape[1]
    nsp = c2s_ref.shape[1]
    hp = lax.Precision.HIGHEST
    nt_dims = (((1,), (1,)), ((), ()))
    r_iota = lax.broadcasted_iota(jnp.int32, (rows, 1), 0)
    qpos_r = past + r_iota % t_len
    lane_g = lax.broadcasted_iota(jnp.int32, (1, KV_DIM), 1) // HEAD_DIM
    row_g = (r_iota // t_len) % N_KV

    def tile_rows(x, k):
        return jnp.concatenate([x] * k, axis=0)

    @pl.when(s == 0)
    def _():
        qf = q_ref[...].astype(F32)
        q128 = jnp.concatenate([tile_rows(qf[:, KV_DIM * h:KV_DIM * (h + 1)], N_KV) for h in range(nh)], axis=0)
        q128 = jnp.where(lane_g == row_g, q128, 0.0).astype(BF16)
        qs_ref[...] = q128
        hd = N_HEADS * HEAD_DIM
        gx = [jnp.concatenate([tile_rows(gate_ref[:, br * hd + KV_DIM * h:br * hd + KV_DIM * (h + 1)].astype(F32), N_KV)
                               for h in range(nh)], axis=0) for br in range(3)]
        new = new_ref[...]
        n_idx = lax.broadcasted_iota(jnp.int32, (1, nc), 1)
        cmp_ok = (n_idx >= 1) & (n_idx <= n_blk) & (n_idx * CMP_STRIDE + (CMP_STRIDE - 1) <= qpos_r)
        s_c = lax.dot_general(q128, kc_ref[0], nt_dims, preferred_element_type=F32)
        p_c = _masked_softmax_rows(s_c, cmp_ok)
        o_c = jnp.dot(p_c.astype(BF16), vc_ref[0], preferred_element_type=F32)
        gq = N_KV * t_len
        pcs = p_c[0:gq]
        for h in range(1, nh):
            pcs = pcs + p_c[h * gq:(h + 1) * gq]
        imp = jnp.dot(pcs, c2s_ref[...], precision=hp, preferred_element_type=F32)
        cur = (past + lax.broadcasted_iota(jnp.int32, (gq, 1), 0) % t_len) // SEL_BLOCK
        sel = _select_blocks(imp, cur, N_SEL, three_forced=past >= 2 * SEL_BLOCK).astype(F32)
        selb = tile_rows(sel, nh).astype(BF16)
        for st in range(pick_ref.shape[0]):
            selb_ref[st] = jnp.dot(selb, pick_ref[st], preferred_element_type=F32).astype(BF16)
        kwin = jnp.concatenate([kwb_ref[0], new[:, 2 * KV_DIM:3 * KV_DIM]], axis=0).astype(BF16)
        vwin = jnp.concatenate([vwb_ref[0], new[:, 3 * KV_DIM:]], axis=0).astype(BF16)
        wb = kwb_ref.shape[1]
        kw_pos = past - wb + lax.broadcasted_iota(jnp.int32, (1, wb + t_len), 1)
        dlt = qpos_r - kw_pos
        win_ok = (dlt >= 0) & (dlt < WINDOW) & (kw_pos >= 0)
        s_w = lax.dot_general(q128, kwin, nt_dims, preferred_element_type=F32)
        p_w = _masked_softmax_rows(s_w, win_ok)
        o_w = jnp.dot(p_w.astype(BF16), vwin, preferred_element_type=F32)
        oo_ref[0] = gx[0] * o_c + gx[2] * o_w
        oo_ref[1] = gx[1]
        knew = new[:, 0:KV_DIM].astype(BF16)
        vnew = new[:, KV_DIM:2 * KV_DIM].astype(BF16)
        s_n = lax.dot_general(q128, knew, nt_dims, preferred_element_type=F32)
        jn = lax.broadcasted_iota(jnp.int32, (nsp, t_len), 0)
        picked = jnp.dot(selb, (jn == past // SEL_BLOCK).astype(BF16), preferred_element_type=F32)
        ok = (picked > 0.5) & (past + lax.broadcasted_iota(jnp.int32, (1, t_len), 1) <= qpos_r)
        s_n = jnp.where(ok, s_n, NEG_BIG)
        m0 = jnp.max(s_n, -1, keepdims=True)
        p0 = jnp.where(ok, jnp.exp2(s_n - m0), 0.0)
        m_ref[...] = m0
        l_ref[...] = jnp.sum(p0, -1, keepdims=True)
        acc_ref[...] = jnp.dot(p0.astype(BF16), vnew, preferred_element_type=F32)

    tk = n_pg * PAGE_SIZE
    qs = qs_ref[...]
    sc = jnp.concatenate([jnp.dot(qs, p[0].astype(BF16), preferred_element_type=F32) for p in kpages], axis=1)
    picked = jnp.dot(selb_ref[s], base_ref[...], preferred_element_type=F32)
    sc = jnp.where(picked > 0.5, sc, NEG_BIG)
    m = m_ref[...]
    m_new = jnp.maximum(m, jnp.max(sc, -1, keepdims=True))
    alpha = jnp.exp2(m - m_new)
    p = jnp.where(picked > 0.5, jnp.exp2(sc - m_new), 0.0)
    l_ref[...] = alpha * l_ref[...] + jnp.sum(p, -1, keepdims=True)
    pb = p.astype(BF16)
    pv = jnp.zeros((rows, KV_DIM), F32)
    for k, vp in enumerate(vpages):
        pv = pv + lax.dot_general(pb[:, k * PAGE_SIZE:(k + 1) * PAGE_SIZE], vp[0].astype(BF16), nt_dims,
                                  preferred_element_type=F32)
    acc_ref[...] = alpha * acc_ref[...] + pv
    m_ref[...] = m_new

    @pl.when(s == pl.num_programs(1) - 1)
    def _():
        l = l_ref[...]
        o_s = acc_ref[...] / jnp.where(l > 0, l, 1.0)
        total = oo_ref[0] + oo_ref[1] * o_s
        gq = N_KV * t_len
        for h in range(nh):
            oh = jnp.zeros((t_len, KV_DIM), F32)
            for g in range(N_KV):
                oh = jnp.where(lane_g == g, total[h * gq + g * t_len:h * gq + (g + 1) * t_len], oh)
            o_ref[:, KV_DIM * h:KV_DIM * (h + 1)] = oh.astype(o_ref.dtype)


def _nsa_attn_sample(q, gates, new, page_table, pool_ks, pool_vs, buf_kw, buf_vw, kc, vc, *, t_len):
    B, n_pages = page_table.shape
    past = n_pages * PAGE_SIZE
    total = past + t_len
    n_blk = -(-total // CMP_STRIDE) - 1
    ns = -(-total // SEL_BLOCK)
    nsp = -(-ns // 128) * 128
    nc = kc.shape[1]
    c2s = jnp.pad(_cmp_to_sel(nc, ns), ((0, 0), (0, nsp - ns)))
    n_pg = min(16, n_pages)
    assert n_pages % n_pg == 0 and t_len % 8 == 0
    steps = n_pages // n_pg
    rows = N_HEADS * t_len
    wb = buf_kw.shape[1]
    one = pl.Buffered(1)
    bps = n_pg * PAGE_SIZE // SEL_BLOCK
    jj = jnp.arange(nsp)[None, :, None]
    cc = jnp.arange(128)[None, None, :]
    pick = ((jj == jnp.arange(steps)[:, None, None] * bps + cc) & (cc < bps)).astype(BF16)
    base = (jnp.arange(128)[:, None] == jnp.arange(n_pg * PAGE_SIZE)[None, :] // SEL_BLOCK).astype(BF16)

    def page_map(b, s, pt, k):
        return (pt[b * n_pages + s * n_pg + k], 0, 0)

    per_b = lambda w: pl.BlockSpec((t_len, w), lambda b, s, pt: (b, 0))
    grid_spec = pltpu.PrefetchScalarGridSpec(
        num_scalar_prefetch=1,
        grid=(B, steps),
        in_specs=[per_b(D_MODEL), per_b(3 * D_MODEL), per_b(4 * KV_DIM)]
        + [pl.BlockSpec((1, KV_DIM, PAGE_SIZE), functools.partial(page_map, k=k)) for k in range(n_pg)] * 2
        + [pl.BlockSpec((1, wb, KV_DIM), lambda b, s, pt: (b, 0, 0)),
           pl.BlockSpec((1, wb, KV_DIM), lambda b, s, pt: (b, 0, 0)),
           pl.BlockSpec((1, nc, KV_DIM), lambda b, s, pt: (b, 0, 0)),
           pl.BlockSpec((1, nc, KV_DIM), lambda b, s, pt: (b, 0, 0)),
           pl.BlockSpec((nc, nsp), lambda b, s, pt: (0, 0), pipeline_mode=one),
           pl.BlockSpec((steps, nsp, 128), lambda b, s, pt: (0, 0, 0), pipeline_mode=one),
           pl.BlockSpec((128, n_pg * PAGE_SIZE), lambda b, s, pt: (0, 0), pipeline_mode=one)],
        out_specs=pl.BlockSpec((t_len, D_MODEL), lambda b, s, pt: (b, 0)),
        scratch_shapes=[pltpu.VMEM((rows, KV_DIM), BF16), pltpu.VMEM((steps, rows, 128), BF16),
                        pltpu.VMEM((rows, 1), F32), pltpu.VMEM((rows, 1), F32), pltpu.VMEM((rows, KV_DIM), F32),
                        pltpu.VMEM((2, rows, KV_DIM), F32)],
    )
    return pl.pallas_call(
        functools.partial(_nsa_attn_sample_kernel, n_pg=n_pg, past=past, t_len=t_len, n_blk=n_blk),
        grid_spec=grid_spec,
        out_shape=jax.ShapeDtypeStruct((B * t_len, D_MODEL), BF16),
        compiler_params=_cparams("parallel", "arbitrary"),
        name="nsa_attn_sample",
    )(page_table.reshape(-1), q, gates, new, *([pool_ks] * n_pg), *([pool_vs] * n_pg), buf_kw, buf_vw, kc, vc, c2s,
      pick, base)


def _nsa_sample_layer(x, page_table, pool_kc, pool_vc, pool_ks, pool_vs, buf_kw, buf_vw,
                      w_in, pe, w1, b1, w2, w_o, g, b, *, t_len):
    B = page_table.shape[0]
    q, k_c, v_c, _, gates, k_s, v_s, k_w, v_w = _nsa_proj(x, w_in, rows_t=False)
    flat = lambda p: p.reshape(p.shape[0], p.shape[1], KV_DIM)
    pages_t = lambda p: p.transpose(0, 2, 3, 1).reshape(p.shape[0], KV_DIM, p.shape[1])
    kc = _compress_sample(pages_t(pool_kc), page_table, k_c.reshape(B, t_len, KV_DIM), _cmp_weights(pe[0], w1[0], b1[0], w2[0]))
    vc = _compress_sample(pages_t(pool_vc), page_table, v_c.reshape(B, t_len, KV_DIM), _cmp_weights(pe[1], w1[1], b1[1], w2[1]))
    new = jnp.concatenate([k_s, v_s, k_w, v_w], axis=1)
    o = _nsa_attn_sample(q, gates, new, page_table, pages_t(pool_ks), pages_t(pool_vs), flat(buf_kw), flat(buf_vw),
                         kc, vc, t_len=t_len)
    y = _out_proj_ln(x, o, w_o, g, b)
    return y, (k_c, v_c, k_s, v_s, k_w, v_w), o


def kernel(x_prompt, x_sample, state_s5_re, state_s5_im, cache_k_cmp, cache_v_cmp, cache_k_sel, cache_v_sel,
           cache_k_win, cache_v_win, state_ffn_conv, page_table,
           s5_a_re, s5_a_im, s5_log_dt, s5_b_re, s5_b_im, s5_c_re, s5_c_im, s5_d, s5_w_glu,
           nsa_w_in, nsa_cmp_pe, nsa_cmp_w1, nsa_cmp_b1, nsa_cmp_w2, nsa_w_o,
           ffn_w_up, ffn_conv_w, ffn_conv_b, ffn_w_down,
           ln_mix_g, ln_mix_b, ln_ffn_g, ln_ffn_b):
    assert s5_a_re.shape[0] == 1 and nsa_w_in.shape[0] == 1 and ffn_w_up.shape[0] == DEPTH
    bsz, L, _ = x_prompt.shape
    B, t_len, _ = x_sample.shape
    xp, xs, st_p, st_s = _s5_layer(x_prompt, x_sample, state_s5_re[0], state_s5_im[0], s5_a_re[0], s5_a_im[0],
                                   s5_log_dt[0], s5_b_re[0], s5_b_im[0], s5_c_re[0], s5_c_im[0], s5_d[0],
                                   s5_w_glu[0], ln_mix_g[0], ln_mix_b[0])
    fw = (ffn_w_up[0], ffn_conv_w[0], ffn_conv_b[0], ffn_w_down[0], ln_ffn_g[0], ln_ffn_b[0])
    xp, cp0 = _ffn(xp, None, *fw, seq_len=L)
    xs, cs0 = _ffn(xs, state_ffn_conv[0], *fw, seq_len=t_len)
    nw = (nsa_w_in[0], nsa_cmp_pe[0], nsa_cmp_w1[0], nsa_cmp_b1[0], nsa_cmp_w2[0], nsa_w_o[0],
          ln_mix_g[1], ln_mix_b[1])
    xp, rows_p = _nsa_prompt_layer(xp, *nw)
    xs, rows_s, _ = _nsa_sample_layer(xs, page_table, cache_k_cmp[0], cache_v_cmp[0], cache_k_sel[0], cache_v_sel[0],
                                      cache_k_win[0], cache_v_win[0], *nw, t_len=t_len)
    fw = (ffn_w_up[1], ffn_conv_w[1], ffn_conv_b[1], ffn_w_down[1], ln_ffn_g[1], ln_ffn_b[1])
    xp, cp1 = _ffn(xp, None, *fw, seq_len=L)
    xs, cs1 = _ffn(xs, state_ffn_conv[1], *fw, seq_len=t_len)

    heads = lambda r, lead: r.reshape(1, lead, -1, N_KV, HEAD_DIM)
    heads_t = lambda r: r.reshape(N_KV, HEAD_DIM, -1).transpose(2, 0, 1)[None, None]
    win = min(WINDOW, L)
    wb = cache_k_win.shape[2]
    keep = min(WINDOW, wb + t_len)
    win_s = lambda buf, new: jnp.concatenate([buf[0], new.reshape(B, t_len, N_KV, HEAD_DIM)], axis=1)[None, :, wb + t_len - keep:]
    return (xp[None], xs.reshape(B, t_len, D_MODEL),
            st_p[0][None], st_p[1][None], st_s[0][None], st_s[1][None],
            heads_t(rows_p[0]), heads_t(rows_p[1]), heads_t(rows_p[2]), heads_t(rows_p[3]),
            heads(rows_s[0], B), heads(rows_s[1], B), heads(rows_s[2], B), heads(rows_s[3], B),
            heads_t(rows_p[4][:, L - win:]), heads_t(rows_p[5][:, L - win:]),
            win_s(cache_k_win, rows_s[4]), win_s(cache_v_win, rows_s[5]),
            jnp.stack([cp0, cp1]), jnp.stack([cs0, cs1]))


def _s5_layer(xp, xs, h_re, h_im, a_re, a_im, log_dt, b_re, b_im, c_re, c_im, d_skip, w_glu, g, b):
    assert xp.shape[0] == 1
    mats = _s5_prep(a_re, a_im, log_dt, b_re, b_im, c_re, c_im)
    P = S5_STATE
    x2p = xp[0]
    yp, hfp = _s5_core_prompt(x2p, mats)
    op = _s5_glu_ln(x2p, yp, d_skip, w_glu, g, b)
    x2s = xs.reshape(-1, D_MODEL)
    ys, hfs = _s5_core_sample(xs, h_re, h_im, mats)
    os_ = _s5_glu_ln(x2s, ys, d_skip, w_glu, g, b)
    st_p = (hfp[None, :, :P], hfp[None, :, P:])
    hfs = hfs.transpose(1, 0, 2)
    st_s = (hfs[:, :, :P], hfs[:, :, P:])
    return op, os_, st_p, st_s
```

```python
import functools
import math

import jax
import jax.numpy as jnp
from jax import lax
from jax.experimental import pallas as pl
from jax.experimental.pallas import tpu as pltpu

F32 = jnp.float32
BF16 = jnp.bfloat16

D_MODEL = 1024
S5_GROUP = 16
S5_GROUPS = D_MODEL // S5_GROUP
S5_STATE = 64
S5_T = 16
S5_SLAB = 128 // S5_GROUP
N_HEADS = 16
HEAD_DIM = 64
N_KV = 4
HEADS_PER_KV = N_HEADS // N_KV
KV_DIM = N_KV * HEAD_DIM
CMP_LEN = 32
CMP_STRIDE = 16
CMP_HIDDEN = 2 * HEAD_DIM
SEL_BLOCK = 64
N_SEL = 16
WINDOW = 512
PAGE_SIZE = 128
FORCE = 1e6
D_FF = 2816
CONV_W = 3
FFN_CHUNK = 2816
DEPTH = 2
ALPHA = (2.0 * DEPTH) ** 0.25
LN_EPS = 1e-5
NEG_BIG = -1e30

VMEM_LIMIT = 56 * 1024 * 1024


def _cparams(*sem):
    return pltpu.CompilerParams(dimension_semantics=sem, vmem_limit_bytes=VMEM_LIMIT)


def _layer_norm(r, g, b):
    mu = jnp.mean(r, -1, keepdims=True)
    c = r - mu
    var = jnp.mean(c * c, -1, keepdims=True)
    return c * lax.rsqrt(var + LN_EPS) * g + b


def _swap(x):
    return pltpu.roll(x, S5_STATE, axis=x.ndim - 1)


def _lo_mask(shape):
    return lax.broadcasted_iota(jnp.int32, shape, len(shape) - 1) < S5_STATE


def _cmul(x, w):
    ws = _swap(w)
    lo = _lo_mask(w.shape)
    wr = jnp.where(lo, w, ws)
    wi = jnp.where(lo, -ws, w)
    return x * wr + _swap(x) * wi


def _s5_prep_kernel(a_ref, dt_ref, bt_ref, btt_ref, ct_ref, m_ref, w_ref, v_ref, z_ref):
    P = S5_STATE
    T = S5_T
    a = a_ref[0]
    dt = jnp.exp(dt_ref[0])
    lo = _lo_mask(a.shape)
    a_sw = _swap(a)
    are = jnp.where(lo, a, a_sw)
    aim = jnp.where(lo, a_sw, a)
    mag = jnp.exp(are * dt)
    ang = aim * dt
    ab = mag * jnp.where(lo, jnp.cos(ang), jnp.sin(ang))
    ab_sw = _swap(ab)
    abr = jnp.where(lo, ab, ab_sw)
    abi = jnp.where(lo, ab_sw, ab)
    den = are * are + aim * aim
    nr = abr - 1.0
    f = jnp.where(lo, nr * are + abi * aim, abi * are - nr * aim) / den

    pw = [ab]
    for _ in range(4):
        pw.append(_cmul(pw[-1], pw[-1]))
    z_ref[0] = jnp.concatenate(
        [jnp.broadcast_to(pw[3], (4, 2 * P)), jnp.broadcast_to(pw[4], (4, 2 * P))], axis=0)

    rows = lax.broadcasted_iota(jnp.int32, (T * S5_GROUP, 2 * P), 0) // S5_GROUP
    e = (T - 1) - rows
    one = jnp.where(_lo_mask((T * S5_GROUP, 2 * P)), 1.0, 0.0).astype(F32)
    apow = one
    for b in range(4):
        apow = jnp.where(((e >> b) & 1) == 1, _cmul(apow, pw[b]), apow)
    af = _cmul(apow, f)
    w_ref[0] = _cmul(af, btt_ref[0])

    col = [jnp.transpose(jnp.broadcast_to(p, (2 * P, 2 * P))) for p in pw[:4]]
    lane_e = lax.broadcasted_iota(jnp.int32, (P, T * S5_GROUP), 1) // S5_GROUP

    def col_cmul(xr, xi, c):
        cr = jnp.concatenate([c[:P], c[:P]], axis=1)
        ci = jnp.concatenate([c[P:], c[P:]], axis=1)
        return xr * cr - xi * ci, xr * ci + xi * cr

    pr = jnp.ones((P, T * S5_GROUP), F32)
    pi = jnp.zeros((P, T * S5_GROUP), F32)
    for b in range(4):
        nr_, ni_ = col_cmul(pr, pi, col[b])
        sel = ((lane_e >> b) & 1) == 1
        pr = jnp.where(sel, nr_, pr)
        pi = jnp.where(sel, ni_, pi)
    ctr = ct_ref[0, 0]
    cti = ct_ref[0, 1]
    qr = pr * ctr - pi * cti
    qi = pr * cti + pi * ctr
    q1r, q1i = col_cmul(qr, qi, col[0])
    v_ref[0] = jnp.concatenate([q1r, -q1i], axis=0).astype(v_ref.dtype)

    fr = jnp.where(lo, f, _swap(f))[:, :P]
    fi = jnp.where(lo, _swap(f), f)[:, :P]
    btr = bt_ref[0, 0]
    bti = bt_ref[0, 1]
    bfr = btr * fr - bti * fi
    bfi = btr * fi + bti * fr
    hp = lax.Precision.HIGHEST
    kk = (jnp.dot(bfr, qr, precision=hp, preferred_element_type=F32)
          - jnp.dot(bfi, qi, precision=hp, preferred_element_type=F32))
    lane = lax.broadcasted_iota(jnp.int32, kk.shape, 1)
    for s in range(T):
        blk = kk if s == 0 else jnp.where(lane >= S5_GROUP * s, pltpu.roll(kk, S5_GROUP * s, axis=1), 0.0)
        m_ref[0, S5_GROUP * s:S5_GROUP * (s + 1), :] = blk.astype(m_ref.dtype)


def _s5_prep(a_re, a_im, log_dt, b_re, b_im, c_re, c_im):
    G, P, T = S5_GROUPS, S5_STATE, S5_T
    a_pk = jnp.concatenate([a_re, a_im], axis=-1).reshape(G, 1, 2 * P)
    dt_pk = jnp.broadcast_to(log_dt.reshape(G, 1, 1), (G, 1, 2 * P))
    bt = jnp.stack([b_re, b_im], axis=1).transpose(0, 1, 3, 2)
    btt = jnp.tile(jnp.concatenate([b_re, b_im], axis=1).transpose(0, 2, 1), (1, T, 1))
    ct = jnp.tile(jnp.stack([c_re, c_im], axis=1).transpose(0, 1, 3, 2), (1, 1, 1, T))
    n = T * S5_GROUP
    return pl.pallas_call(
        _s5_prep_kernel,
        grid=(G,),
        in_specs=[
            pl.BlockSpec((1, 1, 2 * P), lambda g: (g, 0, 0)),
            pl.BlockSpec((1, 1, 2 * P), lambda g: (g, 0, 0)),
            pl.BlockSpec((1, 2, S5_GROUP, P), lambda g: (g, 0, 0, 0)),
            pl.BlockSpec((1, n, 2 * P), lambda g: (g, 0, 0)),
            pl.BlockSpec((1, 2, P, n), lambda g: (g, 0, 0, 0)),
        ],
        out_specs=[
            pl.BlockSpec((1, n, n), lambda g: (g, 0, 0)),
            pl.BlockSpec((1, n, 2 * P), lambda g: (g, 0, 0)),
            pl.BlockSpec((1, 2 * P, n), lambda g: (g, 0, 0)),
            pl.BlockSpec((1, 8, 2 * P), lambda g: (g, 0, 0)),
        ],
        out_shape=[
            jax.ShapeDtypeStruct((G, n, n), BF16),
            jax.ShapeDtypeStruct((G, n, 2 * P), F32),
            jax.ShapeDtypeStruct((G, 2 * P, n), BF16),
            jax.ShapeDtypeStruct((G, 8, 2 * P), F32),
        ],
        compiler_params=_cparams("parallel"),
        name="s5_prep",
    )(a_pk, dt_pk, bt, btt, ct)


def _s5_scan_kernel(x_ref, m_ref, w_ref, v_ref, z_ref, y_ref, hf_ref, yg_ref):
    L = x_ref.shape[0]
    T, GS = S5_T, S5_GROUP
    C = L // T
    per = 128 // GS
    piece = lax.broadcasted_iota(jnp.int32, (1, 128), 1) // GS
    row = lax.broadcasted_iota(jnp.int32, (C, 128), 0)
    n_half = T // per

    def group(gp, carry):
        halves = []
        for hh in range(n_half):
            acc = None
            for tp in range(per):
                xt = x_ref[pl.ds(hh * per + tp, C, stride=T), :]
                r = pltpu.roll(xt, (GS * (tp - gp)) & 127, axis=1)
                acc = r if acc is None else jnp.where(piece == tp, r, acc)
            halves.append(acc)
        u = jnp.concatenate(halves, axis=1).astype(BF16)
        s = jnp.dot(u, w_ref[gp].astype(BF16), preferred_element_type=F32)
        z = z_ref[gp, 4:5, :]
        h = s
        d = 1
        while d < C:
            hs = jnp.where(row >= d, pltpu.roll(h, d, axis=0), 0.0)
            h = h + _cmul(hs, z)
            z = _cmul(z, z)
            d *= 2
        hprev = jnp.where(row >= 1, pltpu.roll(h, 1, axis=0), 0.0)
        y = (jnp.dot(u, m_ref[gp], preferred_element_type=F32)
             + jnp.dot(hprev.astype(BF16), v_ref[gp], preferred_element_type=F32))
        for hh in range(n_half):
            yg_ref[gp, hh] = y[:, 128 * hh:128 * (hh + 1)]
        hf_ref[gp] = h[C - 8:, :]
        return carry

    lax.fori_loop(0, S5_SLAB, group, 0)

    def position(t, carry):
        hh, tp = t // per, t % per
        acc = None
        for gp in range(S5_SLAB):
            r = pltpu.roll(yg_ref[gp, hh], (GS * (gp - tp)) & 127, axis=1)
            acc = r if acc is None else jnp.where(piece == gp, r, acc)
        y_ref[pl.ds(t, C, stride=T), :] = acc
        return carry

    lax.fori_loop(0, T, position, 0)


def _s5_step_kernel(u_ref, h0_ref, m_ref, w_ref, v_ref, z_ref, y_ref, hf_ref, *, t_len):
    n = t_len * S5_GROUP
    off = (S5_T - t_len) * S5_GROUP
    u = u_ref[0]
    h0 = h0_ref[0]
    hp = lax.Precision.HIGHEST
    s = jnp.dot(u, w_ref[0, off:, :], precision=hp, preferred_element_type=F32)
    z = z_ref[0, 0:1, :] if t_len * 2 == S5_T else z_ref[0, 4:5, :]
    hf_ref[0] = _cmul(h0, z) + s
    y = (jnp.dot(u.astype(BF16), m_ref[0, :n, :n], preferred_element_type=F32)
         + jnp.dot(h0.astype(BF16), v_ref[0, :, :n], preferred_element_type=F32))
    y_ref[0] = y.astype(y_ref.dtype)


def _s5_core_prompt(x, mats):
    m, w, v, z = mats
    L = x.shape[0]
    G, T = S5_GROUPS, S5_T
    C = L // T
    n = T * S5_GROUP
    gs = S5_SLAB
    assert C % 8 == 0 and S5_SLAB * S5_GROUP == 128
    y, hf = pl.pallas_call(
        _s5_scan_kernel,
        grid=(G // gs,),
        in_specs=[
            pl.BlockSpec((L, 128), lambda j: (0, j), pipeline_mode=pl.Buffered(1)),
            pl.BlockSpec((gs, n, n), lambda j: (j, 0, 0)),
            pl.BlockSpec((gs, n, 2 * S5_STATE), lambda j: (j, 0, 0)),
            pl.BlockSpec((gs, 2 * S5_STATE, n), lambda j: (j, 0, 0)),
            pl.BlockSpec((gs, 8, 2 * S5_STATE), lambda j: (j, 0, 0)),
        ],
        out_specs=[
            pl.BlockSpec((L, 128), lambda j: (0, j)),
            pl.BlockSpec((gs, 8, 2 * S5_STATE), lambda j: (j, 0, 0)),
        ],
        out_shape=[
            jax.ShapeDtypeStruct((L, D_MODEL), F32),
            jax.ShapeDtypeStruct((G, 8, 2 * S5_STATE), F32),
        ],
        scratch_shapes=[pltpu.VMEM((gs, n // 128, C, 128), F32)],
        compiler_params=_cparams("parallel"),
        name="s5_scan",
    )(x, m, w, v, z)
    return y, hf[:, 7, :]


def _s5_core_sample(x, h_re, h_im, mats):
    m, w, v, z = mats
    B, t_len, _ = x.shape
    assert t_len in (S5_T // 2, S5_T)
    G = S5_GROUPS
    n = t_len * S5_GROUP
    u = x.reshape(B, t_len, G, S5_GROUP).transpose(2, 0, 1, 3).reshape(G, B, n)
    h0 = jnp.concatenate([h_re, h_im], axis=-1).transpose(1, 0, 2)
    nf = S5_T * S5_GROUP
    y, hf = pl.pallas_call(
        functools.partial(_s5_step_kernel, t_len=t_len),
        grid=(G,),
        in_specs=[
            pl.BlockSpec((1, B, n), lambda g: (g, 0, 0)),
            pl.BlockSpec((1, B, 2 * S5_STATE), lambda g: (g, 0, 0)),
            pl.BlockSpec((1, nf, nf), lambda g: (g, 0, 0)),
            pl.BlockSpec((1, nf, 2 * S5_STATE), lambda g: (g, 0, 0)),
            pl.BlockSpec((1, 2 * S5_STATE, nf), lambda g: (g, 0, 0)),
            pl.BlockSpec((1, 8, 2 * S5_STATE), lambda g: (g, 0, 0)),
        ],
        out_specs=[
            pl.BlockSpec((1, B, n), lambda g: (g, 0, 0)),
            pl.BlockSpec((1, B, 2 * S5_STATE), lambda g: (g, 0, 0)),
        ],
        out_shape=[
            jax.ShapeDtypeStruct((G, B, n), BF16),
            jax.ShapeDtypeStruct((G, B, 2 * S5_STATE), F32),
        ],
        compiler_params=_cparams("parallel"),
        name="s5_step",
    )(u, h0, m, w, v, z)
    y = y.reshape(G, B, t_len, S5_GROUP).transpose(1, 2, 0, 3).reshape(B * t_len, D_MODEL)
    return y, hf


def _s5_glu_ln_kernel(x_ref, y_ref, d_ref, w_ref, g_ref, b_ref, o_ref):
    x = x_ref[...]
    yy = y_ref[...].astype(F32) + x * d_ref[...]
    zg = jax.nn.gelu(yy).astype(BF16)
    zz = jnp.dot(zg, w_ref[...], preferred_element_type=F32)
    out = zz[:, :D_MODEL] * jax.nn.sigmoid(zz[:, D_MODEL:])
    o_ref[...] = _layer_norm(ALPHA * x + out, g_ref[...], b_ref[...])


def _row_tile(n, cap):
    t = min(n, cap)
    while n % t:
        t //= 2
    return t


def _s5_glu_ln(x, y, d_skip, w_glu, g, b):
    n = x.shape[0]
    tm = _row_tile(n, 512)
    const = lambda i: (0, 0)
    return pl.pallas_call(
        _s5_glu_ln_kernel,
        grid=(n // tm,),
        in_specs=[
            pl.BlockSpec((tm, D_MODEL), lambda i: (i, 0)),
            pl.BlockSpec((tm, D_MODEL), lambda i: (i, 0)),
            pl.BlockSpec((1, D_MODEL), const),
            pl.BlockSpec((D_MODEL, 2 * D_MODEL), const),
            pl.BlockSpec((1, D_MODEL), const),
            pl.BlockSpec((1, D_MODEL), const),
        ],
        out_specs=pl.BlockSpec((tm, D_MODEL), lambda i: (i, 0)),
        out_shape=jax.ShapeDtypeStruct((n, D_MODEL), F32),
        compiler_params=_cparams("parallel"),
        name="s5_glu_ln",
    )(x, y, d_skip.reshape(1, -1), w_glu.astype(BF16), g.reshape(1, -1), b.reshape(1, -1))


def _ffn_kernel(x_ref, p1_ref, p2_ref, wu_ref, cw_ref, cb_ref, wd_ref, g_ref, b_ref,
                o_ref, st_ref, carry_ref, *, period, chained):
    i = pl.program_id(0)
    x = x_ref[...]
    xb = x.astype(BF16)
    tm = x.shape[0]
    row = lax.broadcasted_iota(jnp.int32, (tm, 1), 0)
    t = row % period
    if chained:
        @pl.when(i == 0)
        def _():
            carry_ref[...] = jnp.zeros_like(carry_ref)

    def conv(cols):
        h = jnp.dot(xb, wu_ref[:, cols], preferred_element_type=F32)
        h1 = jnp.where(t >= 1, pltpu.roll(h, 1, axis=0), 0.0)
        h2 = jnp.where(t >= 2, pltpu.roll(h, 2, axis=0), 0.0)
        if chained:
            c0 = carry_ref[6:7, cols]
            c1 = carry_ref[7:8, cols]
            h1 = h1 + jnp.where(row == 0, c1, 0.0)
            h2 = h2 + jnp.where(row == 0, c0, 0.0) + jnp.where(row == 1, c1, 0.0)
            carry_ref[:, cols] = h[tm - 8:, :]
            st_ref[:, cols] = h[tm - 8:, :]
        else:
            h1 = h1 + p1_ref[:, cols]
            h2 = h2 + p2_ref[:, cols]
            st_ref[:, cols] = h
        return cb_ref[:, cols] + cw_ref[0:1, cols] * h2 + cw_ref[1:2, cols] * h1 + cw_ref[2:3, cols] * h

    f = jnp.zeros((tm, D_MODEL), F32)
    for c in range(D_FF // FFN_CHUNK):
        ca = slice(c * FFN_CHUNK, (c + 1) * FFN_CHUNK)
        cv = slice(D_FF + c * FFN_CHUNK, D_FF + (c + 1) * FFN_CHUNK)
        act = jax.nn.silu(conv(ca)) * conv(cv)
        f = f + jnp.dot(act.astype(BF16), wd_ref[ca, :], preferred_element_type=F32)
    o_ref[...] = _layer_norm(ALPHA * x + f, g_ref[...], b_ref[...])


def _ffn(x, buf, w_up, conv_w, conv_b, w_down, g, b, *, seq_len):
    n = x.shape[0]
    n_seq = n // seq_len
    f2 = 2 * D_FF
    chained = buf is None
    if chained:
        assert n_seq == 1
        tm = _row_tile(n, 512)
        p1 = p2 = jnp.zeros((8, f2), F32)
        pspec = pl.BlockSpec((8, f2), lambda i: (0, 0))
        st_shape, st_spec = (8, f2), pl.BlockSpec((8, f2), lambda i: (0, 0))
    else:
        tm = n
        z = jnp.zeros((n_seq, seq_len, f2), F32)
        p1 = z.at[:, 0].set(buf[:, 1]).reshape(n, f2)
        p2 = z.at[:, 0].set(buf[:, 0]).at[:, 1].set(buf[:, 1]).reshape(n, f2)
        pspec = pl.BlockSpec((tm, f2), lambda i: (i, 0))
        st_shape, st_spec = (n, f2), pl.BlockSpec((tm, f2), lambda i: (i, 0))
    const = lambda i: (0, 0)
    out, st = pl.pallas_call(
        functools.partial(_ffn_kernel, period=seq_len, chained=chained),
        grid=(n // tm,),
        in_specs=[
            pl.BlockSpec((tm, D_MODEL), lambda i: (i, 0)),
            pspec, pspec,
            pl.BlockSpec((D_MODEL, f2), const, pipeline_mode=pl.Buffered(1)),
            pl.BlockSpec((CONV_W, f2), const),
            pl.BlockSpec((1, f2), const),
            pl.BlockSpec((D_FF, D_MODEL), const, pipeline_mode=pl.Buffered(1)),
            pl.BlockSpec((1, D_MODEL), const),
            pl.BlockSpec((1, D_MODEL), const),
        ],
        out_specs=[pl.BlockSpec((tm, D_MODEL), lambda i: (i, 0)), st_spec],
        out_shape=[jax.ShapeDtypeStruct((n, D_MODEL), F32), jax.ShapeDtypeStruct(st_shape, F32)],
        scratch_shapes=[pltpu.VMEM((8, f2), F32)],
        compiler_params=_cparams("arbitrary"),
        name="conv_ffn",
    )(x, p1, p2, w_up.astype(BF16), conv_w, conv_b.reshape(1, -1), w_down.astype(BF16),
      g.reshape(1, -1), b.reshape(1, -1))
    if chained:
        state = st[6:8][None]
    else:
        state = st.reshape(n_seq, seq_len, f2)[:, seq_len - 2:]
    return out, state


Q_SCALE = HEAD_DIM ** -0.5 * math.log2(math.e)
TQ = 128
TK = 512
ATTN_PARTS = 4


def _perm_heads():
    idx = jnp.arange(D_MODEL).reshape(N_KV, HEADS_PER_KV, HEAD_DIM)
    return idx.transpose(1, 0, 2).reshape(-1)


def _nsa_proj_kernel(x_ref, wq_ref, wkv_ref, wg_ref, *refs, rows_t):
    xb = x_ref[...].astype(BF16)
    q = jnp.dot(xb, wq_ref[...], preferred_element_type=F32) * Q_SCALE
    kv = jnp.dot(xb, wkv_ref[...], preferred_element_type=F32)
    gates = jax.nn.sigmoid(jnp.dot(xb, wg_ref[...], preferred_element_type=F32))
    if rows_t:
        wkvt_ref, q_ref, kc_ref, vc_ref, kvb_ref, g_ref = refs[:6]
        kvt = lax.dot_general(wkvt_ref[...], xb, (((1,), (1,)), ((), ())), preferred_element_type=F32)
        for j, r in enumerate(refs[6:]):
            r[...] = kvt[j * KV_DIM:(j + 1) * KV_DIM, :]
    else:
        q_ref, kc_ref, vc_ref, kvb_ref, g_ref = refs[:5]
        for j, r in enumerate(refs[5:]):
            r[...] = kv[:, (j + 2) * KV_DIM:(j + 3) * KV_DIM]
    q_ref[...] = q.astype(BF16)
    kc_ref[...] = kv[:, :KV_DIM]
    vc_ref[...] = kv[:, KV_DIM:2 * KV_DIM]
    kvb_ref[...] = kv[:, 2 * KV_DIM:].astype(BF16)
    g_ref[...] = gates.astype(BF16)


def _gate_columns():
    br = jnp.arange(3)[:, None, None, None]
    h = jnp.arange(HEADS_PER_KV)[None, :, None, None]
    g = jnp.arange(N_KV)[None, None, :, None]
    idx = (g * HEADS_PER_KV + h) * 3 + br + jnp.zeros((1, 1, 1, HEAD_DIM), jnp.int32)
    return idx.reshape(-1)


def _nsa_proj(x, w_in, *, rows_t):
    n = x.shape[0]
    tm = _row_tile(n, 512)
    hd = N_HEADS * HEAD_DIM
    wq = w_in[:, :hd][:, _perm_heads()].astype(BF16)
    wkv = w_in[:, hd:hd + 6 * KV_DIM].astype(BF16)
    wg = w_in[:, hd + 6 * KV_DIM:][:, _gate_columns()].astype(BF16)
    const = lambda i: (0, 0)
    rows = lambda w: pl.BlockSpec((tm, w), lambda i: (i, 0))
    in_specs = [rows(D_MODEL), pl.BlockSpec((D_MODEL, hd), const), pl.BlockSpec((D_MODEL, 6 * KV_DIM), const),
                pl.BlockSpec((D_MODEL, 3 * hd), const)]
    out_specs = [rows(hd), rows(KV_DIM), rows(KV_DIM), rows(4 * KV_DIM), rows(3 * hd)]
    out_shape = [jax.ShapeDtypeStruct((n, hd), BF16), jax.ShapeDtypeStruct((n, KV_DIM), F32),
                 jax.ShapeDtypeStruct((n, KV_DIM), F32), jax.ShapeDtypeStruct((n, 4 * KV_DIM), BF16),
                 jax.ShapeDtypeStruct((n, 3 * hd), BF16)]
    args = [x, wq, wkv, wg]
    if rows_t:
        in_specs.append(pl.BlockSpec((6 * KV_DIM, D_MODEL), const))
        args.append(wkv.T)
        out_specs += [pl.BlockSpec((KV_DIM, tm), lambda i: (0, i))] * 6
        out_shape += [jax.ShapeDtypeStruct((KV_DIM, n), F32)] * 6
    else:
        out_specs += [rows(KV_DIM)] * 4
        out_shape += [jax.ShapeDtypeStruct((n, KV_DIM), F32)] * 4
    return pl.pallas_call(
        functools.partial(_nsa_proj_kernel, rows_t=rows_t),
        grid=(n // tm,),
        in_specs=in_specs,
        out_specs=out_specs,
        out_shape=out_shape,
        compiler_params=_cparams("parallel"),
        name="nsa_proj",
    )(*args)


def _cmp_weights(pe, w1, b1, w2):
    eye = jnp.eye(N_KV, dtype=F32)
    w1b = jnp.einsum('jsdh,ge->jsgdeh', w1.reshape(2, CMP_STRIDE, HEAD_DIM, CMP_HIDDEN), eye)
    w1b = w1b.reshape(2, CMP_STRIDE * KV_DIM, N_KV * CMP_HIDDEN).astype(BF16)
    w2b = jnp.einsum('hd,ge->ghed', w2, eye).reshape(N_KV * CMP_HIDDEN, KV_DIM).astype(BF16)
    peb = jnp.broadcast_to(pe.reshape(2, CMP_STRIDE, 1, HEAD_DIM), (2, CMP_STRIDE, N_KV, HEAD_DIM))
    peb = peb.reshape(2, 1, CMP_STRIDE * KV_DIM)
    b1b = jnp.tile(b1, N_KV).reshape(1, N_KV * CMP_HIDDEN)
    return peb, w1b, b1b, w2b


def _compress_rows(x, carry_ref, pe_ref, w1_ref, b1_ref, w2_ref):
    p0 = jnp.dot((x + pe_ref[0]).astype(BF16), w1_ref[0], preferred_element_type=F32)
    p1 = jnp.dot((x + pe_ref[1]).astype(BF16), w1_ref[1], preferred_element_type=F32)
    row = lax.broadcasted_iota(jnp.int32, (x.shape[0], 1), 0)
    p0s = jnp.where(row == 0, carry_ref[7:8, :], pltpu.roll(p0, 1, axis=0))
    carry_ref[...] = p0[x.shape[0] - 8:, :]
    h = b1_ref[...] + p0s + p1
    return jnp.dot(jax.nn.gelu(h).astype(BF16), w2_ref[...], preferred_element_type=F32)


def _compress_prompt_kernel(x_ref, pe_ref, w1_ref, b1_ref, w2_ref, o_ref, carry_ref):
    @pl.when(pl.program_id(0) == 0)
    def _():
        carry_ref[...] = jnp.zeros_like(carry_ref)
    o_ref[...] = _compress_rows(x_ref[...], carry_ref, pe_ref, w1_ref, b1_ref, w2_ref).astype(o_ref.dtype)


def _compress_prompt(rows, cw):
    peb, w1b, b1b, w2b = cw
    n = rows.shape[0]
    nc = n // CMP_STRIDE
    wide = CMP_STRIDE * KV_DIM
    x = rows.reshape(nc, wide)
    tc = _row_tile(nc, 256)
    return pl.pallas_call(
        _compress_prompt_kernel,
        grid=(nc // tc,),
        in_specs=[pl.BlockSpec((tc, wide), lambda i: (i, 0)),
                  pl.BlockSpec((2, 1, wide), lambda i: (0, 0, 0)),
                  pl.BlockSpec((2, wide, N_KV * CMP_HIDDEN), lambda i: (0, 0, 0)),
                  pl.BlockSpec((1, N_KV * CMP_HIDDEN), lambda i: (0, 0)),
                  pl.BlockSpec((N_KV * CMP_HIDDEN, KV_DIM), lambda i: (0, 0))],
        out_specs=pl.BlockSpec((tc, KV_DIM), lambda i: (i, 0)),
        out_shape=jax.ShapeDtypeStruct((nc, KV_DIM), BF16),
        scratch_shapes=[pltpu.VMEM((8, N_KV * CMP_HIDDEN), F32)],
        compiler_params=_cparams("arbitrary"),
        name="nsa_compress",
    )(x, peb, w1b, b1b, w2b)


def _masked_softmax_rows(s, valid):
    s = jnp.where(valid, s, NEG_BIG)
    m = jnp.max(s, -1, keepdims=True)
    e = jnp.where(valid, jnp.exp2(s - m), 0.0)
    den = jnp.sum(e, -1, keepdims=True)
    return e / jnp.where(den > 0, den, 1.0)


def _biased_softmax_rows(s, bias, row_ok):
    s = s + bias
    m = jnp.max(s, -1, keepdims=True)
    e = jnp.exp2(s - m)
    den = jnp.sum(e, -1, keepdims=True)
    return e * jnp.where(row_ok, 1.0 / den, 0.0)


def _group_lanes(per_group):
    lo = lax.broadcasted_iota(jnp.int32, (1, 128), 1) < HEAD_DIM
    return jnp.concatenate([jnp.where(lo, per_group[2 * c], per_group[2 * c + 1]) for c in range(N_KV // 2)], axis=1)


def _select_blocks(imp, cur, n_sel, axis=1, three_forced=False):
    ns = imp.shape[axis]
    jj = lax.broadcasted_iota(jnp.int32, (1, ns) if axis == 1 else (ns, 1), axis)
    forced = (jj == 0) | ((jj <= cur) & (jj >= cur - 1))
    if three_forced and ns >= n_sel:
        v = jnp.where(jj > cur, -FORCE, jnp.where(forced, -jnp.inf, imp))
        n_sel = n_sel - 3
    else:
        v = jnp.where(jj > cur, -FORCE, jnp.where(forced, FORCE, imp))
    jf = jj.astype(F32)
    for _ in range(min(n_sel, ns)):
        m = jnp.max(v, axis, keepdims=True)
        idx = jnp.min(jnp.where(v == m, jf, float(ns)), axis, keepdims=True)
        v = jnp.where(jf == idx, -jnp.inf, v)
    return v == -jnp.inf


def _nsa_attn_prompt_kernel(q_ref, gate_ref, ks_ref, vs_ref, *rest, i0):
    nwb = WINDOW // TQ + 1
    kw_refs, vw_refs = rest[:nwb], rest[nwb:2 * nwb]
    kc_ref, vc_ref, c2st_ref, o_ref, qg_ref, selb_ref, m_ref, l_ref, acc_ref, oo_ref = rest[2 * nwb:]
    i = pl.program_id(0) + i0
    nh = HEADS_PER_KV
    nc = kc_ref.shape[0]
    ns = c2st_ref.shape[0]
    rows = nh * TQ
    hd = N_HEADS * HEAD_DIM
    q = q_ref[...]
    qh = jnp.concatenate([q[:, KV_DIM * h:KV_DIM * (h + 1)] for h in range(nh)], axis=0)
    lane_g = lax.broadcasted_iota(jnp.int32, (1, KV_DIM), 1) // HEAD_DIM
    qpos_r = i * TQ + lax.broadcasted_iota(jnp.int32, (rows, 1), 0) % TQ
    qpos_q = i * TQ + lax.broadcasted_iota(jnp.int32, (TQ, 1), 0)
    cur_t = (i * TQ + lax.broadcasted_iota(jnp.int32, (1, TQ), 1)) // SEL_BLOCK
    nt = (((i + 1) * TQ + TK - 1) // TK)
    nt_dims = (((1,), (1,)), ((), ()))

    def gate_rows(br):
        return jnp.concatenate([gate_ref[:, br * hd + KV_DIM * h:br * hd + KV_DIM * (h + 1)] for h in range(nh)],
                               axis=0).astype(F32)

    kwin = jnp.concatenate([r[...] for r in kw_refs], axis=0)
    vwin = jnp.concatenate([r[...] for r in vw_refs], axis=0)
    kw_pos = i * TQ - WINDOW + lax.broadcasted_iota(jnp.int32, (1, WINDOW + TQ), 1)
    dlt = qpos_r - kw_pos
    bias_w = jnp.where((dlt >= 0) & (dlt < WINDOW) & (kw_pos >= 0), 0.0, NEG_BIG)
    n_idx = lax.broadcasted_iota(jnp.int32, (1, nc), 1)
    bias_c = jnp.where((n_idx >= 1) & (n_idx * CMP_STRIDE + (CMP_STRIDE - 1) <= qpos_r), 0.0, NEG_BIG)
    row_ok_c = qpos_r >= 2 * CMP_STRIDE - 1
    g_c, g_w = gate_rows(0), gate_rows(2)

    impt = []
    for g in range(N_KV):
        qg = jnp.where(lane_g == g, qh, jnp.zeros_like(qh))
        qg_ref[g * rows:(g + 1) * rows, :] = qg
        s_c = lax.dot_general(qg, kc_ref[...], nt_dims, preferred_element_type=F32)
        p_c = _biased_softmax_rows(s_c, bias_c, row_ok_c)
        o_c = jnp.dot(p_c.astype(BF16), vc_ref[...], preferred_element_type=F32)
        pcs = p_c[0:TQ]
        for h in range(1, nh):
            pcs = pcs + p_c[h * TQ:(h + 1) * TQ]
        acc_i = jnp.zeros((ns, TQ), F32)
        rem = pcs
        for _ in range(3):
            part = rem.astype(BF16)
            rem = rem - part.astype(F32)
            acc_i = acc_i + lax.dot_general(c2st_ref[...], part, nt_dims, preferred_element_type=F32)
        impt.append(acc_i)
        s_w = lax.dot_general(qg, kwin, nt_dims, preferred_element_type=F32)
        p_w = _biased_softmax_rows(s_w, bias_w, True)
        o_w = jnp.dot(p_w.astype(BF16), vwin, preferred_element_type=F32)
        oo = jnp.where(lane_g == g, g_c * o_c + g_w * o_w, oo) if g else g_c * o_c + g_w * o_w
        m_ref[g] = jnp.full((rows, 128), NEG_BIG, F32)
        l_ref[g] = jnp.zeros((rows, 128), F32)
    oo_ref[...] = oo
    acc_ref[...] = jnp.zeros((rows, KV_DIM), F32)
    for g2 in range(0, N_KV, 2):
        sel_t = _select_blocks(jnp.concatenate(impt[g2:g2 + 2], axis=1), jnp.concatenate([cur_t] * 2, axis=1),
                               N_SEL, axis=0).astype(F32)
        for k in range(2):
            g = g2 + k
            selb_ref[g * TQ:(g + 1) * TQ, :] = jnp.transpose(sel_t[:, k * TQ:(k + 1) * TQ]).astype(BF16)

    j_row = lax.broadcasted_iota(jnp.int32, (ns, TK), 0)
    j_of_lane = lax.broadcasted_iota(jnp.int32, (ns, TK), 1) // SEL_BLOCK
    key_lane = lax.broadcasted_iota(jnp.int32, (1, TK), 1)

    def body(kt, carry):
        k0 = pl.multiple_of(kt * TK, TK)
        kblk = ks_ref[pl.ds(k0, TK), :]
        vblk = vs_ref[pl.ds(k0, TK), :]
        onehot = (j_row == kt * (TK // SEL_BLOCK) + j_of_lane).astype(BF16)
        causal = k0 + key_lane <= qpos_q
        alphas = []
        pv = None
        for g in range(N_KV):
            picked = jnp.dot(selb_ref[g * TQ:(g + 1) * TQ, :], onehot, preferred_element_type=F32)
            bias = jnp.where((picked > 0.5) & causal, 0.0, NEG_BIG)
            s = lax.dot_general(qg_ref[g * rows:(g + 1) * rows, :], kblk, nt_dims, preferred_element_type=F32)
            s = s + jnp.concatenate([bias] * nh, axis=0)
            m_prev = m_ref[g]
            m_new = jnp.maximum(m_prev, jnp.max(s, -1, keepdims=True))
            alpha = jnp.exp2(m_prev - m_new)
            p = jnp.exp2(s - jnp.concatenate([m_new] * (TK // 128), axis=1))
            l_ref[g] = alpha * l_ref[g] + jnp.sum(p, -1, keepdims=True)
            m_ref[g] = m_new
            alphas.append(alpha)
            pv_g = jnp.dot(p.astype(BF16), jnp.where(lane_g == g, vblk, jnp.zeros_like(vblk)),
                           preferred_element_type=F32)
            pv = pv_g if pv is None else pv + pv_g
        acc_ref[...] = _group_lanes(alphas) * acc_ref[...] + pv
        return carry

    lax.fori_loop(0, (nt + 1) // 2, lambda kk, c: body(2 * kk + 1, body(2 * kk, c)), 0)

    o_s = acc_ref[...] / _group_lanes([l_ref[g] for g in range(N_KV)])
    total = oo_ref[...] + gate_rows(1) * o_s
    for h in range(nh):
        o_ref[:, KV_DIM * h:KV_DIM * (h + 1)] = total[h * TQ:(h + 1) * TQ].astype(o_ref.dtype)


def _cmp_to_sel(nc, ns):
    blk = jnp.arange(nc)[:, None] - 1
    i = blk * CMP_STRIDE
    j = jnp.arange(ns)[None, :] * SEL_BLOCK
    return ((blk >= 0) & (i < j + SEL_BLOCK) & (i + CMP_LEN > j)).astype(F32)


def _nsa_attn_prompt(q, gates, kvb, kc, vc):
    n = q.shape[0]
    assert n % TK == 0 and WINDOW % TQ == 0
    wpad = jnp.pad(kvb[:, 2 * KV_DIM:], ((WINDOW, 0), (0, 0)))
    nwb = WINDOW // TQ + 1
    rows = HEADS_PER_KV * TQ
    const2 = lambda i: (0, 0)
    one = pl.Buffered(1)
    parts = ATTN_PARTS if n % (ATTN_PARTS * 2 * TK) == 0 else 1
    outs = []
    for p in range(parts):
        n_eff = (p + 1) * (n // parts)
        nc, ns = n_eff // CMP_STRIDE, n_eff // SEL_BLOCK
        i0 = p * (n // parts) // TQ
        c2st = _cmp_to_sel(nc, ns).T.astype(BF16)
        blk = lambda w, k=0, col=0, i0=i0: pl.BlockSpec(
            (TQ, w), functools.partial(lambda i, k, col, i0: (i + i0 + k, col), k=k, col=col, i0=i0))
        in_specs = [
            blk(D_MODEL), blk(3 * D_MODEL),
            pl.BlockSpec((n_eff, KV_DIM), lambda i: (0, 0), pipeline_mode=one),
            pl.BlockSpec((n_eff, KV_DIM), lambda i: (0, 1), pipeline_mode=one),
        ]
        in_specs += [blk(KV_DIM, k, 0) for k in range(nwb)] + [blk(KV_DIM, k, 1) for k in range(nwb)]
        in_specs += [
            pl.BlockSpec((nc, KV_DIM), const2, pipeline_mode=one),
            pl.BlockSpec((nc, KV_DIM), const2, pipeline_mode=one),
            pl.BlockSpec((ns, nc), const2, pipeline_mode=one),
        ]
        outs.append(pl.pallas_call(
            functools.partial(_nsa_attn_prompt_kernel, i0=i0),
            grid=(n // parts // TQ,),
            in_specs=in_specs,
            out_specs=pl.BlockSpec((TQ, D_MODEL), lambda i: (i, 0)),
            out_shape=jax.ShapeDtypeStruct((n // parts, D_MODEL), BF16),
            scratch_shapes=[pltpu.VMEM((N_KV * rows, KV_DIM), BF16), pltpu.VMEM((N_KV * TQ, ns), BF16),
                            pltpu.VMEM((N_KV, rows, 128), F32), pltpu.VMEM((N_KV, rows, 128), F32),
                            pltpu.VMEM((rows, KV_DIM), F32), pltpu.VMEM((rows, KV_DIM), F32)],
            compiler_params=_cparams("parallel"),
            name="nsa_attn_prompt",
        )(q, gates, kvb, kvb, *([wpad] * (2 * nwb)), kc, vc, c2st))
    return outs[0] if parts == 1 else jnp.concatenate(outs, axis=0)


def _out_proj_ln_kernel(x_ref, o_ref, w_ref, g_ref, b_ref, y_ref):
    y = jnp.dot(o_ref[...], w_ref[...], preferred_element_type=F32)
    y_ref[...] = _layer_norm(ALPHA * x_ref[...] + y, g_ref[...], b_ref[...])


def _out_proj_ln(x, o, w_o, g, b):
    n = x.shape[0]
    tm = _row_tile(n, 512)
    const = lambda i: (0, 0)
    return pl.pallas_call(
        _out_proj_ln_kernel,
        grid=(n // tm,),
        in_specs=[pl.BlockSpec((tm, D_MODEL), lambda i: (i, 0)), pl.BlockSpec((tm, D_MODEL), lambda i: (i, 0)),
                  pl.BlockSpec((D_MODEL, D_MODEL), const), pl.BlockSpec((1, D_MODEL), const),
                  pl.BlockSpec((1, D_MODEL), const)],
        out_specs=pl.BlockSpec((tm, D_MODEL), lambda i: (i, 0)),
        out_shape=jax.ShapeDtypeStruct((n, D_MODEL), F32),
        compiler_params=_cparams("parallel"),
        name="nsa_out_ln",
    )(x, o, w_o[_perm_heads(), :].astype(BF16), g.reshape(1, -1), b.reshape(1, -1))


def _nsa_prompt_layer(x, w_in, pe, w1, b1, w2, w_o, g, b):
    q, k_c, v_c, kvb, gates, *rows_t = _nsa_proj(x, w_in, rows_t=True)
    kc = _compress_prompt(k_c, _cmp_weights(pe[0], w1[0], b1[0], w2[0]))
    vc = _compress_prompt(v_c, _cmp_weights(pe[1], w1[1], b1[1], w2[1]))
    o = _nsa_attn_prompt(q, gates, kvb, kc, vc)
    y = _out_proj_ln(x, o, w_o, g, b)
    return y, tuple(rows_t)


def _compress_sample_kernel(pt_ref, *refs, n_pg):
    pages = refs[:n_pg]
    new_ref, pe_ref, w1_ref, b1_ref, w2_ref, o_ref, xa_ref, xb_ref, carry_ref = refs[n_pg:]
    s = pl.program_id(1)
    new_tile = pl.num_programs(1) - 2
    tc = n_pg * (PAGE_SIZE // CMP_STRIDE)
    nl = KV_DIM // 128
    nh = N_KV * CMP_HIDDEN

    @pl.when(s == 0)
    def _():
        xa_ref[...] = jnp.zeros_like(xa_ref)
        xb_ref[...] = jnp.zeros_like(xb_ref)

    @pl.when(s == 1)
    def _():
        carry_ref[...] = jnp.zeros_like(carry_ref)

    def fill(dst_ref):
        for k, p in enumerate(pages):
            xt = jnp.transpose(p[0])
            if k == 0:
                head = jnp.where(s == new_tile, new_ref[0], xt[:CMP_STRIDE])
                xt = jnp.concatenate([head, xt[CMP_STRIDE:]], axis=0)
            for c in range(nl):
                dst_ref[c, k * PAGE_SIZE:(k + 1) * PAGE_SIZE, :] = xt[:, c * 128:(c + 1) * 128]

    def compress(src_ref):
        p0 = jnp.zeros((tc, nh), F32)
        p1 = jnp.zeros((tc, nh), F32)
        for ss in range(CMP_STRIDE):
            x = jnp.concatenate([src_ref[c, pl.ds(ss, tc, stride=CMP_STRIDE), :] for c in range(nl)], axis=1)
            p0 = p0 + jnp.dot((x + pe_ref[0, ss:ss + 1, :]).astype(BF16), w1_ref[0, ss], preferred_element_type=F32)
            p1 = p1 + jnp.dot((x + pe_ref[1, ss:ss + 1, :]).astype(BF16), w1_ref[1, ss], preferred_element_type=F32)
        row = lax.broadcasted_iota(jnp.int32, (tc, 1), 0)
        p0s = jnp.where(row == 0, carry_ref[7:8, :], pltpu.roll(p0, 1, axis=0))
        carry_ref[...] = p0[tc - 8:, :]
        h = b1_ref[...] + p0s + p1
        o_ref[0] = jnp.dot(jax.nn.gelu(h).astype(BF16), w2_ref[...], preferred_element_type=F32).astype(o_ref.dtype)

    @pl.when(s % 2 == 0)
    def _():
        fill(xa_ref)
        compress(xb_ref)

    @pl.when(s % 2 == 1)
    def _():
        fill(xb_ref)
        compress(xa_ref)


def _compress_sample(pool_t, page_table, new_rows, cw):
    peb, w1b, b1b, w2b = cw
    B, n_pages = page_table.shape
    t = new_rows.shape[1]
    assert t <= CMP_STRIDE
    cpp = PAGE_SIZE // CMP_STRIDE
    n_pg = min(16, n_pages)
    assert n_pages % n_pg == 0
    tiles = n_pages // n_pg + 1
    steps = tiles + 1
    nh = N_KV * CMP_HIDDEN
    pe_s = peb.reshape(2, CMP_STRIDE, KV_DIM)
    w1_s = w1b.reshape(2, CMP_STRIDE, KV_DIM, nh)
    newc = jnp.pad(new_rows, ((0, 0), (0, CMP_STRIDE - t), (0, 0)))

    def page_map(b, s, pt, k):
        return (pt[b * n_pages + jnp.minimum(s * n_pg + k, n_pages - 1)], 0, 0)

    tc = n_pg * cpp
    grid_spec = pltpu.PrefetchScalarGridSpec(
        num_scalar_prefetch=1,
        grid=(B, steps),
        in_specs=[pl.BlockSpec((1, KV_DIM, PAGE_SIZE), functools.partial(page_map, k=k)) for k in range(n_pg)] + [
            pl.BlockSpec((1, CMP_STRIDE, KV_DIM), lambda b, s, pt: (b, 0, 0)),
            pl.BlockSpec((2, CMP_STRIDE, KV_DIM), lambda b, s, pt: (0, 0, 0)),
            pl.BlockSpec((2, CMP_STRIDE, KV_DIM, nh), lambda b, s, pt: (0, 0, 0, 0)),
            pl.BlockSpec((1, nh), lambda b, s, pt: (0, 0)),
            pl.BlockSpec((nh, KV_DIM), lambda b, s, pt: (0, 0))],
        out_specs=pl.BlockSpec((1, tc, KV_DIM), lambda b, s, pt: (b, jnp.maximum(s - 1, 0), 0)),
        scratch_shapes=[pltpu.VMEM((KV_DIM // 128, n_pg * PAGE_SIZE, 128), F32),
                        pltpu.VMEM((KV_DIM // 128, n_pg * PAGE_SIZE, 128), F32), pltpu.VMEM((8, nh), F32)],
    )
    return pl.pallas_call(
        functools.partial(_compress_sample_kernel, n_pg=n_pg),
        grid_spec=grid_spec,
        out_shape=jax.ShapeDtypeStruct((B, tiles * tc, KV_DIM), BF16),
        compiler_params=_cparams("parallel", "arbitrary"),
        name="nsa_compress_paged",
    )(page_table.reshape(-1), *([pool_t] * n_pg), newc, pe_s, w1_s, b1b, w2b)


def _nsa_attn_sample_kernel(pt_ref, q_ref, gate_ref, new_ref, *refs, n_pg, past, t_len, n_blk):
    kpages, vpages = refs[:n_pg], refs[n_pg:2 * n_pg]
    (kwb_ref, vwb_ref, kc_ref, vc_ref, c2s_ref, pick_ref, base_ref,
     o_ref, qs_ref, selb_ref, m_ref, l_ref, acc_ref, oo_ref) = refs[2 * n_pg:]
    s = pl.program_id(1)
    nh = HEADS_PER_KV
    rows = nh * N_KV * t_len
    nc = kc_ref.shape[1]
    nsp = c2s_ref.shape[1]
    hp = lax.Precision.HIGHEST
    nt_dims = (((1,), (1,)), ((), ()))
    r_iota = lax.broadcasted_iota(jnp.int32, (rows, 1), 0)
    qpos_r = past + r_iota % t_len
    lane_g = lax.broadcasted_iota(jnp.int32, (1, KV_DIM), 1) // HEAD_DIM
    row_g = (r_iota // t_len) % N_KV

    def tile_rows(x, k):
        return jnp.concatenate([x] * k, axis=0)

    @pl.when(s == 0)
    def _():
        qf = q_ref[...].astype(F32)
        q128 = jnp.concatenate([tile_rows(qf[:, KV_DIM * h:KV_DIM * (h + 1)], N_KV) for h in range(nh)], axis=0)
        q128 = jnp.where(lane_g == row_g, q128, 0.0).astype(BF16)
        qs_ref[...] = q128
        hd = N_HEADS * HEAD_DIM
        gx = [jnp.concatenate([tile_rows(gate_ref[:, br * hd + KV_DIM * h:br * hd + KV_DIM * (h + 1)].astype(F32), N_KV)
                               for h in range(nh)], axis=0) for br in range(3)]
        new = new_ref[...]
        n_idx = lax.broadcasted_iota(jnp.int32, (1, nc), 1)
        cmp_ok = (n_idx >= 1) & (n_idx <= n_blk) & (n_idx * CMP_STRIDE + (CMP_STRIDE - 1) <= qpos_r)
        s_c = lax.dot_general(q128, kc_ref[0], nt_dims, preferred_element_type=F32)
        p_c = _masked_softmax_rows(s_c, cmp_ok)
        o_c = jnp.dot(p_c.astype(BF16), vc_ref[0], preferred_element_type=F32)
        gq = N_KV * t_len
        pcs = p_c[0:gq]
        for h in range(1, nh):
            pcs = pcs + p_c[h * gq:(h + 1) * gq]
        imp = jnp.dot(pcs, c2s_ref[...], precision=hp, preferred_element_type=F32)
        cur = (past + lax.broadcasted_iota(jnp.int32, (gq, 1), 0) % t_len) // SEL_BLOCK
        sel = _select_blocks(imp, cur, N_SEL, three_forced=past >= 2 * SEL_BLOCK).astype(F32)
        selb = tile_rows(sel, nh).astype(BF16)
        for st in range(pick_ref.shape[0]):
            selb_ref[st] = jnp.dot(selb, pick_ref[st], preferred_element_type=F32).astype(BF16)
        kwin = jnp.concatenate([kwb_ref[0], new[:, 2 * KV_DIM:3 * KV_DIM]], axis=0).astype(BF16)
        vwin = jnp.concatenate([vwb_ref[0], new[:, 3 * KV_DIM:]], axis=0).astype(BF16)
        wb = kwb_ref.shape[1]
        kw_pos = past - wb + lax.broadcasted_iota(jnp.int32, (1, wb + t_len), 1)
        dlt = qpos_r - kw_pos
        win_ok = (dlt >= 0) & (dlt < WINDOW) & (kw_pos >= 0)
        s_w = lax.dot_general(q128, kwin, nt_dims, preferred_element_type=F32)
        p_w = _masked_softmax_rows(s_w, win_ok)
        o_w = jnp.dot(p_w.astype(BF16), vwin, preferred_element_type=F32)
        oo_ref[0] = gx[0] * o_c + gx[2] * o_w
        oo_ref[1] = gx[1]
        knew = new[:, 0:KV_DIM].astype(BF16)
        vnew = new[:, KV_DIM:2 * KV_DIM].astype(BF16)
        s_n = lax.dot_general(q128, knew, nt_dims, preferred_element_type=F32)
        jn = lax.broadcasted_iota(jnp.int32, (nsp, t_len), 0)
        picked = jnp.dot(selb, (jn == past // SEL_BLOCK).astype(BF16), preferred_element_type=F32)
        ok = (picked > 0.5) & (past + lax.broadcasted_iota(jnp.int32, (1, t_len), 1) <= qpos_r)
        s_n = jnp.where(ok, s_n, NEG_BIG)
        m0 = jnp.max(s_n, -1, keepdims=True)
        p0 = jnp.where(ok, jnp.exp2(s_n - m0), 0.0)
        m_ref[...] = m0
        l_ref[...] = jnp.sum(p0, -1, keepdims=True)
        acc_ref[...] = jnp.dot(p0.astype(BF16), vnew, preferred_element_type=F32)

    tk = n_pg * PAGE_SIZE
    qs = qs_ref[...]
    sc = jnp.concatenate([jnp.dot(qs, p[0].astype(BF16), preferred_element_type=F32) for p in kpages], axis=1)
    picked = jnp.dot(selb_ref[s], base_ref[...], preferred_element_type=F32)
    sc = jnp.where(picked > 0.5, sc, NEG_BIG)
    m = m_ref[...]
    m_new = jnp.maximum(m, jnp.max(sc, -1, keepdims=True))
    alpha = jnp.exp2(m - m_new)
    p = jnp.where(picked > 0.5, jnp.exp2(sc - m_new), 0.0)
    l_ref[...] = alpha * l_ref[...] + jnp.sum(p, -1, keepdims=True)
    pb = p.astype(BF16)
    pv = jnp.zeros((rows, KV_DIM), F32)
    for k, vp in enumerate(vpages):
        pv = pv + lax.dot_general(pb[:, k * PAGE_SIZE:(k + 1) * PAGE_SIZE], vp[0].astype(BF16), nt_dims,
                                  preferred_element_type=F32)
    acc_ref[...] = alpha * acc_ref[...] + pv
    m_ref[...] = m_new

    @pl.when(s == pl.num_programs(1) - 1)
    def _():
        l = l_ref[...]
        o_s = acc_ref[...] / jnp.where(l > 0, l, 1.0)
        total = oo_ref[0] + oo_ref[1] * o_s
        gq = N_KV * t_len
        for h in range(nh):
            oh = jnp.zeros((t_len, KV_DIM), F32)
            for g in range(N_KV):
                oh = jnp.where(lane_g == g, total[h * gq + g * t_len:h * gq + (g + 1) * t_len], oh)
            o_ref[:, KV_DIM * h:KV_DIM * (h + 1)] = oh.astype(o_ref.dtype)


def _nsa_attn_sample(q, gates, new, page_table, pool_ks, pool_vs, buf_kw, buf_vw, kc, vc, *, t_len):
    B, n_pages = page_table.shape
    past = n_pages * PAGE_SIZE
    total = past + t_len
    n_blk = -(-total // CMP_STRIDE) - 1
    ns = -(-total // SEL_BLOCK)
    nsp = -(-ns // 128) * 128
    nc = kc.shape[1]
    c2s = jnp.pad(_cmp_to_sel(nc, ns), ((0, 0), (0, nsp - ns)))
    n_pg = min(16, n_pages)
    assert n_pages % n_pg == 0 and t_len % 8 == 0
    steps = n_pages // n_pg
    rows = N_HEADS * t_len
    wb = buf_kw.shape[1]
    one = pl.Buffered(1)
    bps = n_pg * PAGE_SIZE // SEL_BLOCK
    jj = jnp.arange(nsp)[None, :, None]
    cc = jnp.arange(128)[None, None, :]
    pick = ((jj == jnp.arange(steps)[:, None, None] * bps + cc) & (cc < bps)).astype(BF16)
    base = (jnp.arange(128)[:, None] == jnp.arange(n_pg * PAGE_SIZE)[None, :] // SEL_BLOCK).astype(BF16)

    def page_map(b, s, pt, k):
        return (pt[b * n_pages + s * n_pg + k], 0, 0)

    per_b = lambda w: pl.BlockSpec((t_len, w), lambda b, s, pt: (b, 0))
    grid_spec = pltpu.PrefetchScalarGridSpec(
        num_scalar_prefetch=1,
        grid=(B, steps),
        in_specs=[per_b(D_MODEL), per_b(3 * D_MODEL), per_b(4 * KV_DIM)]
        + [pl.BlockSpec((1, KV_DIM, PAGE_SIZE), functools.partial(page_map, k=k)) for k in range(n_pg)] * 2
        + [pl.BlockSpec((1, wb, KV_DIM), lambda b, s, pt: (b, 0, 0)),
           pl.BlockSpec((1, wb, KV_DIM), lambda b, s, pt: (b, 0, 0)),
           pl.BlockSpec((1, nc, KV_DIM), lambda b, s, pt: (b, 0, 0)),
           pl.BlockSpec((1, nc, KV_DIM), lambda b, s, pt: (b, 0, 0)),
           pl.BlockSpec((nc, nsp), lambda b, s, pt: (0, 0), pipeline_mode=one),
           pl.BlockSpec((steps, nsp, 128), lambda b, s, pt: (0, 0, 0), pipeline_mode=one),
           pl.BlockSpec((128, n_pg * PAGE_SIZE), lambda b, s, pt: (0, 0), pipeline_mode=one)],
        out_specs=pl.BlockSpec((t_len, D_MODEL), lambda b, s, pt: (b, 0)),
        scratch_shapes=[pltpu.VMEM((rows, KV_DIM), BF16), pltpu.VMEM((steps, rows, 128), BF16),
                        pltpu.VMEM((rows, 1), F32), pltpu.VMEM((rows, 1), F32), pltpu.VMEM((rows, KV_DIM), F32),
                        pltpu.VMEM((2, rows, KV_DIM), F32)],
    )
    return pl.pallas_call(
        functools.partial(_nsa_attn_sample_kernel, n_pg=n_pg, past=past, t_len=t_len, n_blk=n_blk),
        grid_spec=grid_spec,
        out_shape=jax.ShapeDtypeStruct((B * t_len, D_MODEL), BF16),
        compiler_params=_cparams("parallel", "arbitrary"),
        name="nsa_attn_sample",
    )(page_table.reshape(-1), q, gates, new, *([pool_ks] * n_pg), *([pool_vs] * n_pg), buf_kw, buf_vw, kc, vc, c2s,
      pick, base)


def _nsa_sample_layer(x, page_table, pool_kc, pool_vc, pool_ks, pool_vs, buf_kw, buf_vw,
                      w_in, pe, w1, b1, w2, w_o, g, b, *, t_len):
    B = page_table.shape[0]
    q, k_c, v_c, _, gates, k_s, v_s, k_w, v_w = _nsa_proj(x, w_in, rows_t=False)
    flat = lambda p: p.reshape(p.shape[0], p.shape[1], KV_DIM)
    pages_t = lambda p: p.transpose(0, 2, 3, 1).reshape(p.shape[0], KV_DIM, p.shape[1])
    kc = _compress_sample(pages_t(pool_kc), page_table, k_c.reshape(B, t_len, KV_DIM), _cmp_weights(pe[0], w1[0], b1[0], w2[0]))
    vc = _compress_sample(pages_t(pool_vc), page_table, v_c.reshape(B, t_len, KV_DIM), _cmp_weights(pe[1], w1[1], b1[1], w2[1]))
    new = jnp.concatenate([k_s, v_s, k_w, v_w], axis=1)
    o = _nsa_attn_sample(q, gates, new, page_table, pages_t(pool_ks), pages_t(pool_vs), flat(buf_kw), flat(buf_vw),
                         kc, vc, t_len=t_len)
    y = _out_proj_ln(x, o, w_o, g, b)
    return y, (k_c, v_c, k_s, v_s, k_w, v_w), o


def kernel(x_prompt, x_sample, state_s5_re, state_s5_im, cache_k_cmp, cache_v_cmp, cache_k_sel, cache_v_sel,
           cache_k_win, cache_v_win, state_ffn_conv, page_table,
           s5_a_re, s5_a_im, s5_log_dt, s5_b_re, s5_b_im, s5_c_re, s5_c_im, s5_d, s5_w_glu,
           nsa_w_in, nsa_cmp_pe, nsa_cmp_w1, nsa_cmp_b1, nsa_cmp_w2, nsa_w_o,
           ffn_w_up, ffn_conv_w, ffn_conv_b, ffn_w_down,
           ln_mix_g, ln_mix_b, ln_ffn_g, ln_ffn_b):
    assert s5_a_re.shape[0] == 1 and nsa_w_in.shape[0] == 1 and ffn_w_up.shape[0] == DEPTH
    bsz, L, _ = x_prompt.shape
    B, t_len, _ = x_sample.shape
    xp, xs, st_p, st_s = _s5_layer(x_prompt, x_sample, state_s5_re[0], state_s5_im[0], s5_a_re[0], s5_a_im[0],
                                   s5_log_dt[0], s5_b_re[0], s5_b_im[0], s5_c_re[0], s5_c_im[0], s5_d[0],
                                   s5_w_glu[0], ln_mix_g[0], ln_mix_b[0])
    fw = (ffn_w_up[0], ffn_conv_w[0], ffn_conv_b[0], ffn_w_down[0], ln_ffn_g[0], ln_ffn_b[0])
    xp, cp0 = _ffn(xp, None, *fw, seq_len=L)
    xs, cs0 = _ffn(xs, state_ffn_conv[0], *fw, seq_len=t_len)
    nw = (nsa_w_in[0], nsa_cmp_pe[0], nsa_cmp_w1[0], nsa_cmp_b1[0], nsa_cmp_w2[0], nsa_w_o[0],
          ln_mix_g[1], ln_mix_b[1])
    xp, rows_p = _nsa_prompt_layer(xp, *nw)
    xs, rows_s, _ = _nsa_sample_layer(xs, page_table, cache_k_cmp[0], cache_v_cmp[0], cache_k_sel[0], cache_v_sel[0],
                                      cache_k_win[0], cache_v_win[0], *nw, t_len=t_len)
    fw = (ffn_w_up[1], ffn_conv_w[1], ffn_conv_b[1], ffn_w_down[1], ln_ffn_g[1], ln_ffn_b[1])
    xp, cp1 = _ffn(xp, None, *fw, seq_len=L)
    xs, cs1 = _ffn(xs, state_ffn_conv[1], *fw, seq_len=t_len)

    heads = lambda r, lead: r.reshape(1, lead, -1, N_KV, HEAD_DIM)
    heads_t = lambda r: r.reshape(N_KV, HEAD_DIM, -1).transpose(2, 0, 1)[None, None]
    win = min(WINDOW, L)
    wb = cache_k_win.shape[2]
    keep = min(WINDOW, wb + t_len)
    win_s = lambda buf, new: jnp.concatenate([buf[0], new.reshape(B, t_len, N_KV, HEAD_DIM)], axis=1)[None, :, wb + t_len - keep:]
    return (xp[None], xs.reshape(B, t_len, D_MODEL),
            st_p[0][None], st_p[1][None], st_s[0][None], st_s[1][None],
            heads_t(rows_p[0]), heads_t(rows_p[1]), heads_t(rows_p[2]), heads_t(rows_p[3]),
            heads(rows_s[0], B), heads(rows_s[1], B), heads(rows_s[2], B), heads(rows_s[3], B),
            heads_t(rows_p[4][:, L - win:]), heads_t(rows_p[5][:, L - win:]),
            win_s(cache_k_win, rows_s[4]), win_s(cache_v_win, rows_s[5]),
            jnp.stack([cp0, cp1]), jnp.stack([cs0, cs1]))


def _s5_layer(xp, xs, h_re, h_im, a_re, a_im, log_dt, b_re, b_im, c_re, c_im, d_skip, w_glu, g, b):
    assert xp.shape[0] == 1
    mats = _s5_prep(a_re, a_im, log_dt, b_re, b_im, c_re, c_im)
    P = S5_STATE
    x2p = xp[0]
    yp, hfp = _s5_core_prompt(x2p, mats)
    op = _s5_glu_ln(x2p, yp, d_skip, w_glu, g, b)
    x2s = xs.reshape(-1, D_MODEL)
    ys, hfs = _s5_core_sample(xs, h_re, h_im, mats)
    os_ = _s5_glu_ln(x2s, ys, d_skip, w_glu, g, b)
    st_p = (hfp[None, :, :P], hfp[None, :, P:])
    hfs = hfs.transpose(1, 0, 2)
    st_s = (hfs[:, :, :P], hfs[:, :, P:])
    return op, os_, st_p, st_s
```
